```python
import jax, jax.numpy as jnp
from jax import lax
import numpy as np

D_MODEL = 1024
BATCH = 8
SEQ = 2048
DEPTH = 2

GRID_W = 64
CTX_LEN = 256
D_MIX = D_MODEL

A_HEADS = 6
A_HEAD_DIM = 64
A_WIDTH = A_HEADS * A_HEAD_DIM
DECAY_LORA = 64
ICL_LORA = 64
GATE_LORA = 128
LOG_DECAY_SCALE = 0.606531
GN_EPS = A_HEAD_DIM * 1e-5

B_HEADS = 6
QK_NOPE = 64
QK_ROPE = 32
V_HEAD = 64
B_WIDTH = B_HEADS * V_HEAD
Q_LORA = 768
KV_LORA = 256
ROPE_BASE = 10000.0
Q_BLOCK = 128
ATTN_SCALE = (QK_NOPE + QK_ROPE) ** -0.5

C_GROUPS = 4
C_WIDTH = D_MIX - A_WIDTH - B_WIDTH
CONV_W = 3

D_FF = -(-8 * D_MODEL // (3 * 256)) * 256

A_IN = 3 * A_WIDTH + DECAY_LORA + ICL_LORA + GATE_LORA
B_IN = Q_LORA + KV_LORA + QK_ROPE
C_IN = 3 * C_WIDTH
P_IN = A_IN + B_IN + C_IN
EPS = 1e-6

kernel_name = 'hybrid_rwkv7_mla_shortconv_prefix_dit'


def rms_norm(x, g):
    xf = x.astype(jnp.float32)
    y = xf * lax.rsqrt(jnp.mean(xf * xf, axis=-1, keepdims=True) + EPS)
    return (y * g.astype(jnp.float32)).astype(x.dtype)


def shift_prev(u):
    return jnp.pad(u, ((0, 0), (1, 0), (0, 0)))[:, :-1]


def shift_next(u):
    return jnp.pad(u, ((0, 0), (0, 1), (0, 0)))[:, 1:]


def adaln_params(cond, ada_w, ada_b):
    return jnp.split(jax.nn.silu(cond) @ ada_w + ada_b, 6, axis=-1)


def modulated_norm(x, g, shift, scale):
    return rms_norm(x, g) * (1 + scale) + shift


def axial_rope_tables(rows, dtype):
    r_pos, c_pos = jnp.meshgrid(jnp.arange(rows, dtype=jnp.float32),
                                jnp.arange(GRID_W, dtype=jnp.float32), indexing='ij')
    axis_dim = QK_ROPE // 2
    inv_freq = 1.0 / (ROPE_BASE ** (jnp.arange(0, axis_dim, 2, dtype=jnp.float32) / axis_dim))
    ang_r = r_pos.reshape(-1)[:, None] * inv_freq
    ang_c = c_pos.reshape(-1)[:, None] * inv_freq
    return tuple(t.astype(dtype) for t in (jnp.cos(ang_r), jnp.sin(ang_r), jnp.cos(ang_c), jnp.sin(ang_c)))


def rope_axis(x, cos, sin):
    half = x.shape[-1] // 2
    x1, x2 = x[..., :half], x[..., half:]
    return jnp.concatenate([x1 * cos - x2 * sin, x2 * cos + x1 * sin], axis=-1)


def rope_2d(x, tabs):
    cos_r, sin_r, cos_c, sin_c = tabs
    half = QK_ROPE // 2
    return jnp.concatenate([rope_axis(x[..., :half], cos_r, sin_r),
                            rope_axis(x[..., half:], cos_c, sin_c)], axis=-1)


def wkv7_scan(r, w, k, v, a_vec, b_vec, reverse):
    bsz, _, h, n = r.shape
    xs = tuple(jnp.moveaxis(t.astype(jnp.float32), 1, 0) for t in (r, w, k, v, a_vec, b_vec))

    def step(s, inp):
        r_t, w_t, k_t, v_t, a_t, b_t = inp
        sa = jnp.einsum('bhvk,bhk->bhv', s, a_t)
        s = (s * w_t[:, :, None, :] + sa[..., :, None] * b_t[:, :, None, :]
             + v_t[..., :, None] * k_t[:, :, None, :])
        return s, jnp.einsum('bhvk,bhk->bhv', s, r_t)

    s0 = jnp.zeros((bsz, h, n, n), jnp.float32)
    _, y = lax.scan(step, s0, xs, reverse=reverse)
    return jnp.moveaxis(y, 0, 1)


def rwkv7_group(p_ctx, p_lat, tshift_mu, decay_w0, decay_up, icl_a0, icl_up, gate_up,
                k_k, k_a, r_k, lnx_g, lnx_b):
    n_ctx = p_ctx.shape[1]

    def token_shift(p):
        return p + tshift_mu[0] * (shift_prev(p) - p) + tshift_mu[1] * (shift_next(p) - p)

    p = jnp.concatenate([token_shift(p_ctx), token_shift(p_lat)], axis=1)
    bsz, t_all = p.shape[:2]

    def heads(u):
        return u.reshape(bsz, t_all, A_HEADS, A_HEAD_DIM)

    o1, o2, o3 = A_WIDTH, 2 * A_WIDTH, 3 * A_WIDTH
    o4 = o3 + DECAY_LORA
    o5 = o4 + ICL_LORA
    r, k, v = p[..., :o1], p[..., o1:o2], p[..., o2:o3]
    w_lo, a_lo, g_lo = p[..., o3:o4], p[..., o4:o5], p[..., o5:]
    kk = heads(k * k_k).astype(jnp.float32)
    kk = kk * lax.rsqrt(jnp.sum(kk * kk, axis=-1, keepdims=True) + 1e-12)
    rh, vh = heads(r), heads(v)

    def direction(d):
        w = jnp.exp(-LOG_DECAY_SCALE * jax.nn.sigmoid(
            (decay_w0[d] + jnp.tanh(w_lo) @ decay_up[d]).astype(jnp.float32)))
        a = jax.nn.sigmoid((icl_a0[d] + a_lo @ icl_up[d]).astype(jnp.float32))
        k_d = k * (1 + (a.astype(k.dtype) - 1) * k_a)
        return heads(w), heads(k_d), -kk, kk * heads(a)

    w_f, k_f, a_f, b_f = direction(0)
    y_f = wkv7_scan(rh, w_f, k_f, vh, a_f, b_f, reverse=False)

    def to_bwd(u):
        return jnp.concatenate([u[:, n_ctx:], u[:, :n_ctx]], axis=1)

    def from_bwd(u):
        n_lat = t_all - n_ctx
        return jnp.concatenate([u[:, n_lat:], u[:, :n_lat]], axis=1)

    w_b, k_b, a_b, b_b = direction(1)
    y_b = from_bwd(wkv7_scan(*(to_bwd(u) for u in (rh, w_b, k_b, vh, a_b, b_b)), reverse=True))

    y = y_f + y_b
    mu = jnp.mean(y, axis=-1, keepdims=True)
    var = jnp.mean(jnp.square(y - mu), axis=-1, keepdims=True)
    y = ((y - mu) * lax.rsqrt(var + GN_EPS)).reshape(bsz, t_all, A_WIDTH) * lnx_g + lnx_b
    bonus = (jnp.sum(rh * (k_f + k_b) * r_k, axis=-1, keepdims=True) * vh).reshape(bsz, t_all, A_WIDTH)
    gate = jax.nn.sigmoid(g_lo) @ gate_up
    out = (y.astype(p.dtype) + bonus) * gate
    return out[:, :n_ctx], out[:, n_ctx:]


def mla_attend(q_n, q_r, k_n, k_r, v):
    s = jnp.einsum('bqhd,bkhd->bhqk', q_n, k_n) + jnp.einsum('bqhr,bkr->bhqk', q_r, k_r)
    p = jax.nn.softmax(s.astype(jnp.float32) * ATTN_SCALE, axis=-1).astype(v.dtype)
    return jnp.einsum('bhqk,bkhd->bqhd', p, v)


def mla_group(p_lat, p_ctx, q_norm_g, kv_norm_g, w_uq, w_ukv, q_nope_g, k_nope_g, q_rope_g, k_rope_g,
              rope_tabs, need_ctx):
    def queries(p):
        bsz, t = p.shape[:2]
        q = (rms_norm(p[..., :Q_LORA], q_norm_g) @ w_uq).reshape(bsz, t, B_HEADS, QK_NOPE + QK_ROPE)
        return rms_norm(q[..., :QK_NOPE], q_nope_g), rms_norm(q[..., QK_NOPE:], q_rope_g)

    def keys_values(p):
        bsz, t = p.shape[:2]
        kv = (rms_norm(p[..., Q_LORA:Q_LORA + KV_LORA], kv_norm_g) @ w_ukv).reshape(
            bsz, t, B_HEADS, QK_NOPE + V_HEAD)
        k_r = rms_norm(p[..., Q_LORA + KV_LORA:], k_rope_g)
        return rms_norm(kv[..., :QK_NOPE], k_nope_g), k_r, kv[..., QK_NOPE:]

    bsz, n = p_lat.shape[:2]
    n_ctx = p_ctx.shape[1]
    q_n, q_r = queries(p_lat)
    q_r = rope_2d(q_r, tuple(t[:, None, :] for t in rope_tabs))
    k_n, k_r, v = keys_values(p_lat)
    k_r = rope_2d(k_r, rope_tabs)
    kc_n, kc_r, vc = keys_values(p_ctx)
    k_all_n = jnp.concatenate([k_n, kc_n], axis=1)
    k_all_r = jnp.concatenate([k_r, kc_r], axis=1)
    v_all = jnp.concatenate([v, vc], axis=1)

    nb = n // Q_BLOCK

    def blocks(t):
        return jnp.swapaxes(t.reshape((bsz, nb, Q_BLOCK) + t.shape[2:]), 0, 1)

    out = lax.map(lambda qs: mla_attend(qs[0], qs[1], k_all_n, k_all_r, v_all), (blocks(q_n), blocks(q_r)))
    y_lat = jnp.swapaxes(out, 0, 1).reshape(bsz, n, B_WIDTH)
    y_ctx = None
    if need_ctx:
        qc_n, qc_r = queries(p_ctx)
        y_ctx = mla_attend(qc_n, qc_r, kc_n, kc_r, vc).reshape(bsz, n_ctx, B_WIDTH)
    return y_lat, y_ctx


def short_conv_group(p, conv_w):
    b_gate, c_gate, h = p[..., :C_WIDTH], p[..., C_WIDTH:2 * C_WIDTH], p[..., 2 * C_WIDTH:]
    u = c_gate * h
    conv = conv_w[0] * shift_prev(u) + conv_w[1] * u + conv_w[2] * shift_next(u)
    return b_gate * conv


def swiglu(h, w_in, w_out):
    gate, up = jnp.split(h @ w_in, 2, axis=-1)
    return (jax.nn.silu(gate) * up) @ w_out


def setup_inputs(seed: int = 0) -> dict:
    key = jax.random.key(seed)
    ks = iter(jax.random.split(key, 40))
    f32 = jnp.float32

    def nrm(shape, scale):
        return jax.random.normal(next(ks), shape, f32) * scale

    def gain(shape):
        return 1.0 + nrm(shape, 0.02)

    return {
        'x': nrm((BATCH, SEQ, D_MODEL), 1.0),
        'c': nrm((BATCH, D_MODEL), 1.0),
        'ctx': nrm((BATCH, CTX_LEN, D_MODEL), 1.0),
        'c_ctx': nrm((D_MODEL,), 1.0),
        'ada_w': nrm((DEPTH, D_MODEL, 6 * D_MODEL), 0.5 * D_MODEL ** -0.5),
        'ada_b': nrm((DEPTH, 6 * D_MODEL), 0.02),
        'norm1_g': gain((DEPTH, D_MODEL)),
        'norm2_g': gain((DEPTH, D_MODEL)),
        'w_in': nrm((DEPTH, D_MODEL, P_IN), D_MODEL ** -0.5),
        'tshift_mu': jax.random.uniform(next(ks), (DEPTH, 2, A_IN), f32, 0.0, 0.5),
        'decay_w0': nrm((DEPTH, 2, A_WIDTH), 0.5),
        'decay_up': nrm((DEPTH, 2, DECAY_LORA, A_WIDTH), 0.5 * DECAY_LORA ** -0.5),
        'icl_a0': nrm((DEPTH, 2, A_WIDTH), 0.5),
        'icl_up': nrm((DEPTH, 2, ICL_LORA, A_WIDTH), 0.5 * ICL_LORA ** -0.5),
        'gate_up': nrm((DEPTH, GATE_LORA, A_WIDTH), GATE_LORA ** -0.5),
        'k_k': 0.85 + nrm((DEPTH, A_WIDTH), 0.05),
        'k_a': 1.0 + nrm((DEPTH, A_WIDTH), 0.05),
        'r_k': nrm((DEPTH, A_HEADS, A_HEAD_DIM), 0.1),
        'lnx_g': gain((DEPTH, A_WIDTH)),
        'lnx_b': nrm((DEPTH, A_WIDTH), 0.02),
        'q_norm_g': gain((DEPTH, Q_LORA)),
        'kv_norm_g': gain((DEPTH, KV_LORA)),
        'w_uq': nrm((DEPTH, Q_LORA, B_HEADS * (QK_NOPE + QK_ROPE)), Q_LORA ** -0.5),
        'w_ukv': nrm((DEPTH, KV_LORA, B_HEADS * (QK_NOPE + V_HEAD)), KV_LORA ** -0.5),
        'q_nope_g': gain((DEPTH, QK_NOPE)),
        'k_nope_g': gain((DEPTH, QK_NOPE)),
        'q_rope_g': gain((DEPTH, QK_ROPE)),
        'k_rope_g': gain((DEPTH, QK_ROPE)),
        'conv_w': nrm((DEPTH, CONV_W, C_WIDTH), CONV_W ** -0.5),
        'w_out': nrm((DEPTH, D_MIX, D_MODEL), D_MIX ** -0.5),
        'w_ffn_in': nrm((DEPTH, D_MODEL, 2 * D_FF), D_MODEL ** -0.5),
        'w_ffn_out': nrm((DEPTH, D_FF, D_MODEL), D_FF ** -0.5),
    }


def reference(x, c, ctx, c_ctx, ada_w, ada_b, norm1_g, norm2_g, w_in, tshift_mu, decay_w0, decay_up,
              icl_a0, icl_up, gate_up, k_k, k_a, r_k, lnx_g, lnx_b, q_norm_g, kv_norm_g, w_uq, w_ukv,
              q_nope_g, k_nope_g, q_rope_g, k_rope_g, conv_w, w_out, w_ffn_in, w_ffn_out):
    n = x.shape[1]
    ROWS = n // GRID_W
    rope_tabs = axial_rope_tables(ROWS, x.dtype)
    for l in range(DEPTH):
        need_ctx = l < DEPTH - 1
        sh_a, sc_a, g_a, sh_f, sc_f, g_f = adaln_params(c[:, None, :], ada_w[l], ada_b[l])
        csh_a, csc_a, cg_a, csh_f, csc_f, cg_f = adaln_params(c_ctx[None, None, :], ada_w[l], ada_b[l])
        p_lat = modulated_norm(x, norm1_g[l], sh_a, sc_a) @ w_in[l]
        p_ctx = modulated_norm(ctx, norm1_g[l], csh_a, csc_a) @ w_in[l]
        ya_ctx, ya_lat = rwkv7_group(p_ctx[..., :A_IN], p_lat[..., :A_IN], tshift_mu[l], decay_w0[l],
                                     decay_up[l], icl_a0[l], icl_up[l], gate_up[l], k_k[l], k_a[l], r_k[l],
                                     lnx_g[l], lnx_b[l])
        yb_lat, yb_ctx = mla_group(p_lat[..., A_IN:A_IN + B_IN], p_ctx[..., A_IN:A_IN + B_IN], q_norm_g[l],
                                   kv_norm_g[l], w_uq[l], w_ukv[l], q_nope_g[l], k_nope_g[l], q_rope_g[l],
                                   k_rope_g[l], rope_tabs, need_ctx)
        yc_lat = short_conv_group(p_lat[..., A_IN + B_IN:], conv_w[l])
        x = x + g_a * (jnp.concatenate([ya_lat, yb_lat, yc_lat], axis=-1) @ w_out[l])
        x = x + g_f * swiglu(modulated_norm(x, norm2_g[l], sh_f, sc_f), w_ffn_in[l], w_ffn_out[l])
        if need_ctx:
            yc_ctx = short_conv_group(p_ctx[..., A_IN + B_IN:], conv_w[l])
            ctx = ctx + cg_a * (jnp.concatenate([ya_ctx, yb_ctx, yc_ctx], axis=-1) @ w_out[l])
            ctx = ctx + cg_f * swiglu(modulated_norm(ctx, norm2_g[l], csh_f, csc_f), w_ffn_in[l], w_ffn_out[l])
    return x
```

```python
import functools

import jax
import jax.numpy as jnp
from jax import lax
from jax.experimental import pallas as pl
from jax.experimental.pallas import tpu as pltpu

D_MODEL = 1024
GRID_W = 64
A_HEADS = 6
A_HEAD_DIM = 64
A_WIDTH = A_HEADS * A_HEAD_DIM
DECAY_LORA = 64
ICL_LORA = 64
GATE_LORA = 128
LOG_DECAY_SCALE = 0.606531
GN_EPS = A_HEAD_DIM * 1e-5
B_HEADS = 6
QK_NOPE = 64
QK_ROPE = 32
V_HEAD = 64
B_WIDTH = B_HEADS * V_HEAD
Q_LORA = 768
KV_LORA = 256
ROPE_BASE = 10000.0
ATTN_SCALE = (QK_NOPE + QK_ROPE) ** -0.5
C_WIDTH = D_MODEL - A_WIDTH - B_WIDTH
A_IN = 3 * A_WIDTH + DECAY_LORA + ICL_LORA + GATE_LORA
B_IN = Q_LORA + KV_LORA + QK_ROPE
C_IN = 3 * C_WIDTH
EPS = 1e-6

LANES = 128
SUBLANES = 8
ROPE_PAD = LANES
VMEM_LIMIT = 48 * 1024 * 1024


def _mm_kernel(x_ref, w_ref, o_ref):
    x = x_ref[...].astype(jnp.bfloat16)
    w = w_ref[...].astype(jnp.bfloat16)
    o_ref[...] = jnp.dot(x, w, preferred_element_type=jnp.float32).astype(o_ref.dtype)


def _pick_block(n, target, align):
    best = None
    for d in range(align, min(n, target) + 1, align):
        if n % d == 0:
            best = d
    return n if best is None else best


def pmm(x, w, out_dtype=jnp.float32, bm_target=512, bn_target=1024):
    m, k = x.shape
    n = w.shape[1]
    bm = _pick_block(m, bm_target, SUBLANES)
    bn = _pick_block(n, bn_target, LANES)
    return pl.pallas_call(
        _mm_kernel,
        out_shape=jax.ShapeDtypeStruct((m, n), out_dtype),
        grid=(m // bm, n // bn),
        in_specs=[pl.BlockSpec((bm, k), lambda i, j: (i, 0)),
                  pl.BlockSpec((k, bn), lambda i, j: (0, j))],
        out_specs=pl.BlockSpec((bm, bn), lambda i, j: (i, j)),
        compiler_params=pltpu.CompilerParams(
            dimension_semantics=("arbitrary", "arbitrary"), vmem_limit_bytes=VMEM_LIMIT),
        name="pmm",
    )(x, w)


WKV_TB = 16


def _wkv_kernel(a_ref, w_ref, b_ref, k_ref, rp_ref, v_ref, y_ref, s_ref):
    @pl.when(pl.program_id(0) == 0)
    def _():
        s_ref[...] = jnp.zeros_like(s_ref)

    def step(t, carry):
        a = a_ref[t]
        w = w_ref[t]
        b = b_ref[t]
        k = k_ref[t]
        rp = rp_ref[t]
        for v in range(A_HEAD_DIM):
            s = s_ref[v]
            sa = jnp.sum(s * a, axis=0, keepdims=True)
            y = jnp.sum(s * rp, axis=0, keepdims=True)
            vv = v_ref[t, v:v + 1, :]
            s_ref[v] = s * w + sa * b + vv * k
            y_ref[t, v:v + 1, :] = y
        return carry

    lax.fori_loop(0, WKV_TB, step, 0)


def wkv_scan(a, w, b, k, rp, v):
    t_all = a.shape[0]
    spec = pl.BlockSpec((WKV_TB, A_HEAD_DIM, LANES), lambda i: (i, 0, 0))
    return pl.pallas_call(
        _wkv_kernel,
        out_shape=jax.ShapeDtypeStruct((t_all, A_HEAD_DIM, LANES), jnp.float32),
        grid=(t_all // WKV_TB,),
        in_specs=[spec] * 6,
        out_specs=spec,
        scratch_shapes=[pltpu.VMEM((A_HEAD_DIM, A_HEAD_DIM, LANES), jnp.float32)],
        compiler_params=pltpu.CompilerParams(
            dimension_semantics=("arbitrary",), vmem_limit_bytes=VMEM_LIMIT),
        name="wkv_scan",
    )(a, w, b, k, rp, v)


ATT_BQ = 256


def _attn_kernel(q_ref, k_ref, v_ref, o_ref):
    q = q_ref[0, 0]
    k = k_ref[0, 0]
    s = lax.dot_general(q, k, (((1,), (1,)), ((), ())), preferred_element_type=jnp.float32)
    m = jnp.max(s, axis=-1, keepdims=True)
    p = jnp.exp(s - m)
    l = jnp.sum(p, axis=-1, keepdims=True)
    o = jnp.dot(p.astype(jnp.bfloat16), v_ref[0, 0], preferred_element_type=jnp.float32)
    o_ref[0, 0] = o / l


def attention(q, k, v):
    bsz, h, nq, dk = q.shape
    nk = k.shape[2]
    dv = v.shape[3]
    bq = min(ATT_BQ, nq)
    return pl.pallas_call(
        _attn_kernel,
        out_shape=jax.ShapeDtypeStruct((bsz, h, nq, dv), jnp.float32),
        grid=(bsz, h, nq // bq),
        in_specs=[pl.BlockSpec((1, 1, bq, dk), lambda b, hh, i: (b, hh, i, 0)),
                  pl.BlockSpec((1, 1, nk, dk), lambda b, hh, i: (b, hh, 0, 0)),
                  pl.BlockSpec((1, 1, nk, dv), lambda b, hh, i: (b, hh, 0, 0))],
        out_specs=pl.BlockSpec((1, 1, bq, dv), lambda b, hh, i: (b, hh, i, 0)),
        compiler_params=pltpu.CompilerParams(
            dimension_semantics=("arbitrary", "arbitrary", "arbitrary"), vmem_limit_bytes=VMEM_LIMIT),
        name="mla_attention",
    )(q, k, v)


def _rms(x, g):
    return x * lax.rsqrt(jnp.mean(x * x, axis=-1, keepdims=True) + EPS) * g


def _shift_prev(u):
    return jnp.pad(u, ((0, 0), (1, 0), (0, 0)))[:, :-1]


def _shift_next(u):
    return jnp.pad(u, ((0, 0), (0, 1), (0, 0)))[:, 1:]


def _rope_tables(n):
    rows = n // GRID_W
    r_pos, c_pos = jnp.meshgrid(jnp.arange(rows, dtype=jnp.float32),
                                jnp.arange(GRID_W, dtype=jnp.float32), indexing='ij')
    axis_dim = QK_ROPE // 2
    inv_freq = 1.0 / (ROPE_BASE ** (jnp.arange(0, axis_dim, 2, dtype=jnp.float32) / axis_dim))
    ang_r = r_pos.reshape(-1)[:, None] * inv_freq
    ang_c = c_pos.reshape(-1)[:, None] * inv_freq
    return jnp.cos(ang_r), jnp.sin(ang_r), jnp.cos(ang_c), jnp.sin(ang_c)


def _rope_axis(x, cos, sin):
    half = x.shape[-1] // 2
    x1, x2 = x[..., :half], x[..., half:]
    return jnp.concatenate([x1 * cos - x2 * sin, x2 * cos + x1 * sin], axis=-1)


def _rope_2d(x, tabs):
    cos_r, sin_r, cos_c, sin_c = tabs
    half = QK_ROPE // 2
    return jnp.concatenate([_rope_axis(x[..., :half], cos_r, sin_r),
                            _rope_axis(x[..., half:], cos_c, sin_c)], axis=-1)


def _mm3(x, w, **kw):
    bsz, t, k = x.shape
    return pmm(x.reshape(bsz * t, k), w, **kw).reshape(bsz, t, -1)


def _rwkv_group(p, n_ctx, tshift_mu, decay_w0, decay_up, icl_a0, icl_up, gate_up,
                k_k, k_a, r_k, lnx_g, lnx_b):
    bsz, t_all, _ = p.shape

    def token_shift(q):
        return q + tshift_mu[0] * (_shift_prev(q) - q) + tshift_mu[1] * (_shift_next(q) - q)

    p = jnp.concatenate([token_shift(p[:, :n_ctx]), token_shift(p[:, n_ctx:])], axis=1)

    def heads(u):
        return u.reshape(bsz, t_all, A_HEADS, A_HEAD_DIM)

    o1, o2, o3 = A_WIDTH, 2 * A_WIDTH, 3 * A_WIDTH
    o4 = o3 + DECAY_LORA
    o5 = o4 + ICL_LORA
    r, k, v = p[..., :o1], p[..., o1:o2], p[..., o2:o3]
    w_lo, a_lo, g_lo = p[..., o3:o4], p[..., o4:o5], p[..., o5:]
    kk = heads(k * k_k)
    kk = kk * lax.rsqrt(jnp.sum(kk * kk, axis=-1, keepdims=True) + 1e-12)
    rh, vh = heads(r), heads(v)

    decay_all = _mm3(jnp.tanh(w_lo), jnp.concatenate([decay_up[0], decay_up[1]], axis=1))
    icl_all = _mm3(a_lo, jnp.concatenate([icl_up[0], icl_up[1]], axis=1))
    gate = _mm3(jax.nn.sigmoid(g_lo), gate_up)

    def to_bwd(u):
        return jnp.concatenate([u[:, :n_ctx][:, ::-1], u[:, n_ctx:][:, ::-1]], axis=1)

    scan_in = {n: [] for n in ("a", "w", "b", "k", "rp", "v")}
    k_dirs, kr_dirs = [], []
    for d in range(2):
        w = jnp.exp(-LOG_DECAY_SCALE * jax.nn.sigmoid(
            decay_w0[d] + decay_all[..., d * A_WIDTH:(d + 1) * A_WIDTH]))
        icl = jax.nn.sigmoid(icl_a0[d] + icl_all[..., d * A_WIDTH:(d + 1) * A_WIDTH])
        k_d = heads(k * (1 + (icl - 1) * k_a))
        wh = heads(w)
        a_vec = -kk
        b_vec = kk * heads(icl)
        rp = wh * rh + a_vec * jnp.sum(b_vec * rh, axis=-1, keepdims=True)
        kr_dirs.append(jnp.sum(k_d * rh, axis=-1, keepdims=True))
        k_dirs.append(k_d)
        order = to_bwd if d == 1 else (lambda u: u)
        for n, u in (("a", a_vec), ("w", wh), ("b", b_vec), ("k", k_d), ("rp", rp), ("v", vh)):
            scan_in[n].append(order(u))

    n_chain = 2 * bsz * A_HEADS

    def to_lanes(us):
        u = jnp.stack(us, axis=0)
        u = jnp.transpose(u, (2, 4, 0, 1, 3)).reshape(t_all, A_HEAD_DIM, n_chain)
        return jnp.pad(u, ((0, 0), (0, 0), (0, LANES - n_chain)))

    yp = wkv_scan(*(to_lanes(scan_in[n]) for n in ("a", "w", "b", "k", "rp", "v")))
    yp = yp[:, :, :n_chain].reshape(t_all, A_HEAD_DIM, 2, bsz, A_HEADS)
    yp = jnp.transpose(yp, (2, 3, 0, 4, 1))
    y = yp[0] + to_bwd(yp[1]) + vh * (kr_dirs[0] + kr_dirs[1])

    mu = jnp.mean(y, axis=-1, keepdims=True)
    var = jnp.mean(jnp.square(y - mu), axis=-1, keepdims=True)
    y = ((y - mu) * lax.rsqrt(var + GN_EPS)).reshape(bsz, t_all, A_WIDTH) * lnx_g + lnx_b
    bonus = (jnp.sum(rh * (k_dirs[0] + k_dirs[1]) * r_k, axis=-1, keepdims=True) * vh).reshape(
        bsz, t_all, A_WIDTH)
    return (y + bonus) * gate


def _mla_group(p_q, p_kv, p_kr, n_ctx, q_norm_g, kv_norm_g, w_uq, w_ukv, q_nope_g, k_nope_g,
               q_rope_g, k_rope_g, rope_tabs, need_ctx):
    bsz, t_all, _ = p_q.shape
    n_lat = t_all - n_ctx
    q = _mm3(_rms(p_q, q_norm_g), w_uq).reshape(bsz, t_all, B_HEADS, QK_NOPE + QK_ROPE)
    q_n = _rms(q[..., :QK_NOPE], q_nope_g)
    q_r = _rms(q[..., QK_NOPE:], q_rope_g)
    kv = _mm3(_rms(p_kv, kv_norm_g), w_ukv).reshape(bsz, t_all, B_HEADS, QK_NOPE + V_HEAD)
    k_n = _rms(kv[..., :QK_NOPE], k_nope_g)
    v = kv[..., QK_NOPE:]
    k_r = _rms(p_kr, k_rope_g)

    tabs_q = tuple(t[:, None, :] for t in rope_tabs)
    q_r = jnp.concatenate([q_r[:, :n_ctx], _rope_2d(q_r[:, n_ctx:], tabs_q)], axis=1)
    k_r = jnp.concatenate([k_r[:, :n_ctx], _rope_2d(k_r[:, n_ctx:], rope_tabs)], axis=1)

    qf = jnp.concatenate([q_n, q_r], axis=-1) * ATTN_SCALE
    kf = jnp.concatenate([k_n, jnp.broadcast_to(k_r[:, :, None, :], (bsz, t_all, B_HEADS, QK_ROPE))], axis=-1)
    qf = jnp.transpose(qf, (0, 2, 1, 3)).astype(jnp.bfloat16)
    kf = jnp.transpose(kf, (0, 2, 1, 3)).astype(jnp.bfloat16)
    vf = jnp.transpose(v, (0, 2, 1, 3)).astype(jnp.bfloat16)

    y_lat = attention(qf[:, :, n_ctx:], kf, vf)
    y_lat = jnp.transpose(y_lat, (0, 2, 1, 3)).reshape(bsz, n_lat, B_WIDTH)
    y_ctx = None
    if need_ctx:
        y_ctx = attention(qf[:, :, :n_ctx], kf[:, :, :n_ctx], vf[:, :, :n_ctx])
        y_ctx = jnp.transpose(y_ctx, (0, 2, 1, 3)).reshape(bsz, n_ctx, B_WIDTH)
    return y_lat, y_ctx


def _short_conv(p, conv_w):
    b_gate, c_gate, h = p[..., :C_WIDTH], p[..., C_WIDTH:2 * C_WIDTH], p[..., 2 * C_WIDTH:]
    u = c_gate * h
    conv = conv_w[0] * _shift_prev(u) + conv_w[1] * u + conv_w[2] * _shift_next(u)
    return b_gate * conv


def _swiglu(h, w_in, w_out):
    d_ff = w_out.shape[0]
    gu = _mm3(h, w_in, out_dtype=jnp.float32)
    act = (jax.nn.silu(gu[..., :d_ff]) * gu[..., d_ff:]).astype(jnp.bfloat16)
    return _mm3(act, w_out)


def kernel(x, c, ctx, c_ctx, ada_w, ada_b, norm1_g, norm2_g, w_in, tshift_mu, decay_w0, decay_up, icl_a0, icl_up, gate_up, k_k, k_a, r_k, lnx_g, lnx_b, q_norm_g, kv_norm_g, w_uq, w_ukv, q_nope_g, k_nope_g, q_rope_g, k_rope_g, conv_w, w_out, w_ffn_in, w_ffn_out):
    bsz, n, _ = x.shape
    n_ctx = ctx.shape[1]
    depth = ada_w.shape[0]
    rope_tabs = _rope_tables(n)
    bf = jnp.bfloat16

    oq = A_IN
    okv = oq + Q_LORA
    okr = okv + KV_LORA
    oc = okr + ROPE_PAD

    cond = jnp.concatenate([c, c_ctx[None, :], jnp.zeros((2 * SUBLANES - bsz - 1, D_MODEL), c.dtype)], axis=0)
    for l in range(depth):
        need_ctx = l < depth - 1
        ada = pmm(jax.nn.silu(cond), ada_w[l]) + ada_b[l]
        sh_a, sc_a, g_a, sh_f, sc_f, g_f = (u[:bsz, None, :] for u in jnp.split(ada, 6, axis=-1))
        csh_a, csc_a, cg_a, csh_f, csc_f, cg_f = (u[bsz:bsz + 1, None, :] for u in jnp.split(ada, 6, axis=-1))

        wl = w_in[l]
        w_in_p = jnp.concatenate(
            [wl[:, :A_IN + Q_LORA + KV_LORA], wl[:, A_IN + Q_LORA + KV_LORA:A_IN + B_IN],
             jnp.zeros((D_MODEL, ROPE_PAD - QK_ROPE), wl.dtype), wl[:, A_IN + B_IN:]], axis=1).astype(bf)

        h_all = jnp.concatenate([_rms(ctx, norm1_g[l]) * (1 + csc_a) + csh_a,
                                 _rms(x, norm1_g[l]) * (1 + sc_a) + sh_a], axis=1)
        p = _mm3(h_all, w_in_p)

        ya = _rwkv_group(p[..., :A_IN], n_ctx, tshift_mu[l], decay_w0[l], decay_up[l], icl_a0[l], icl_up[l],
                         gate_up[l], k_k[l], k_a[l], r_k[l], lnx_g[l], lnx_b[l])
        yb_lat, yb_ctx = _mla_group(p[..., oq:okv], p[..., okv:okr], p[..., okr:okr + QK_ROPE], n_ctx,
                                    q_norm_g[l], kv_norm_g[l], w_uq[l], w_ukv[l], q_nope_g[l], k_nope_g[l],
                                    q_rope_g[l], k_rope_g[l], rope_tabs, need_ctx)
        pc = p[..., oc:]
        yc_lat = _short_conv(pc[:, n_ctx:], conv_w[l])

        w_out_b = w_out[l].astype(bf)
        w_fi = w_ffn_in[l].astype(bf)
        w_fo = w_ffn_out[l].astype(bf)
        mix = jnp.concatenate([ya[:, n_ctx:], yb_lat, yc_lat], axis=-1)
        x = x + g_a * _mm3(mix, w_out_b)
        x = x + g_f * _swiglu(_rms(x, norm2_g[l]) * (1 + sc_f) + sh_f, w_fi, w_fo)
        if need_ctx:
            yc_ctx = _short_conv(pc[:, :n_ctx], conv_w[l])
            mix_c = jnp.concatenate([ya[:, :n_ctx], yb_ctx, yc_ctx], axis=-1)
            ctx = ctx + cg_a * _mm3(mix_c, w_out_b)
            ctx = ctx + cg_f * _swiglu(_rms(ctx, norm2_g[l]) * (1 + csc_f) + csh_f, w_fi, w_fo)
    return x
```

```python
import functools

import jax
import jax.numpy as jnp
from jax import lax
from jax.experimental import pallas as pl
from jax.experimental.pallas import tpu as pltpu

D_MODEL = 1024
GRID_W = 64
A_HEADS = 6
A_HEAD_DIM = 64
A_WIDTH = A_HEADS * A_HEAD_DIM
DECAY_LORA = 64
ICL_LORA = 64
GATE_LORA = 128
LOG_DECAY_SCALE = 0.606531
GN_EPS = A_HEAD_DIM * 1e-5
B_HEADS = 6
QK_NOPE = 64
QK_ROPE = 32
V_HEAD = 64
B_WIDTH = B_HEADS * V_HEAD
Q_LORA = 768
KV_LORA = 256
ROPE_BASE = 10000.0
ATTN_SCALE = (QK_NOPE + QK_ROPE) ** -0.5
C_WIDTH = D_MODEL - A_WIDTH - B_WIDTH
A_IN = 3 * A_WIDTH + DECAY_LORA + ICL_LORA + GATE_LORA
B_IN = Q_LORA + KV_LORA + QK_ROPE
C_IN = 3 * C_WIDTH
EPS = 1e-6

LANES = 128
SUBLANES = 8
HEAD_PAD = LANES
QP_WIDTH = B_HEADS * HEAD_PAD
KV_IN_PAD = KV_LORA + LANES
P_PAD = A_IN + Q_LORA + KV_IN_PAD + C_IN
VMEM_LIMIT = 48 * 1024 * 1024
BM = 256
F32 = jnp.float32
BF16 = jnp.bfloat16


def _const_spec(shape, nargs):
    zeros = tuple(0 for _ in shape)
    if nargs == 1:
        return pl.BlockSpec(shape, lambda i: zeros)
    if nargs == 2:
        return pl.BlockSpec(shape, lambda i, j: zeros)
    return pl.BlockSpec(shape, lambda i, j, k: zeros)


def _params(n_axes):
    return pltpu.CompilerParams(dimension_semantics=("arbitrary",) * n_axes, vmem_limit_bytes=VMEM_LIMIT)


def _rms_rows(x, g):
    return x * lax.rsqrt(jnp.mean(x * x, axis=-1, keepdims=True) + EPS) * g


def _mm_kernel(x_ref, w_ref, o_ref):
    o_ref[...] = jnp.dot(x_ref[...].astype(BF16), w_ref[...].astype(BF16), preferred_element_type=F32)


def pmm(x, w, bn=1024):
    m, k = x.shape
    n = w.shape[1]
    return pl.pallas_call(
        _mm_kernel,
        out_shape=jax.ShapeDtypeStruct((m, n), F32),
        grid=(n // bn,),
        in_specs=[pl.BlockSpec((m, k), lambda j: (0, 0)), pl.BlockSpec((k, bn), lambda j: (0, j))],
        out_specs=pl.BlockSpec((m, bn), lambda j: (0, j)),
        compiler_params=_params(1),
        name="pmm",
    )(x, w)


def _inproj_kernel(x_ref, mod_ref, g_ref, w_ref, pa_o, pq_o, pkv_o, pc_o):
    sh = mod_ref[0, :, 0:D_MODEL]
    sc = mod_ref[0, :, D_MODEL:2 * D_MODEL]
    h = (_rms_rows(x_ref[0], g_ref[...]) * (1.0 + sc) + sh).astype(BF16)
    o = 0
    for out in (pa_o, pq_o, pkv_o, pc_o):
        n = out.shape[-1]
        out[0] = jnp.dot(h, w_ref[:, o:o + n], preferred_element_type=F32)
        o += n


def inproj(xa, mod, g, w_p):
    bsz, t_all, d = xa.shape
    nblk = t_all // BM
    widths = (A_IN, Q_LORA, KV_IN_PAD, C_IN)
    return pl.pallas_call(
        _inproj_kernel,
        out_shape=[jax.ShapeDtypeStruct((bsz, t_all, n), F32) for n in widths],
        grid=(bsz, nblk),
        in_specs=[pl.BlockSpec((1, BM, d), lambda b, j: (b, j, 0)),
                  pl.BlockSpec((1, 1, 6 * d), lambda b, j: (2 * b + jnp.minimum(j, 1), 0, 0)),
                  _const_spec((1, d), 2), _const_spec((d, P_PAD), 2)],
        out_specs=[pl.BlockSpec((1, BM, n), lambda b, j: (b, j, 0)) for n in widths],
        compiler_params=_params(2),
        name="inproj",
    )(xa, mod, g[None, :], w_p)


def _seg_matrix():
    h = jnp.arange(A_WIDTH) // A_HEAD_DIM
    return (h[:, None] == h[None, :]).astype(BF16)


def _segsum(z, seg):
    hi = z.astype(BF16)
    lo = (z - hi.astype(F32)).astype(BF16)
    return jnp.dot(hi, seg, preferred_element_type=F32) + jnp.dot(lo, seg, preferred_element_type=F32)


RW_TT = 128


def _rwkv_prep_kernel(p_ref, pprev_ref, pnext_ref, mu_ref, wlo_ref, wg_ref, bias_ref, ka_ref, rk_ref, seg_ref,
                      r_o, k_o, v_o, df_o, db_o, if_o, ib_o, vk_o, bonus_o, gate_o, *, n_ctx, t_all):
    t0 = pl.program_id(1) * RW_TT
    x = p_ref[0]
    first = jnp.logical_or(t0 == 0, t0 == n_ctx)
    last = jnp.logical_or(t0 + RW_TT == n_ctx, t0 + RW_TT == t_all)
    prev_row = jnp.where(first, 0.0, pprev_ref[0, SUBLANES - 1:SUBLANES, :])
    next_row = jnp.where(last, 0.0, pnext_ref[0, 0:1, :])
    rows = lax.broadcasted_iota(jnp.int32, x.shape, 0)
    xp = jnp.where(rows == 0, prev_row, pltpu.roll(x, 1, axis=0))
    xn = jnp.where(rows == RW_TT - 1, next_row, pltpu.roll(x, RW_TT - 1, axis=0))
    xs = x + mu_ref[0:1, :] * (xp - x) + mu_ref[1:2, :] * (xn - x)

    o1, o2, o3 = A_WIDTH, 2 * A_WIDTH, 3 * A_WIDTH
    o5 = o3 + DECAY_LORA + ICL_LORA
    r, k, v = xs[:, :o1], xs[:, o1:o2], xs[:, o2:o3]
    lo2 = xs[:, o3:o5]
    lane = lax.broadcasted_iota(jnp.int32, lo2.shape, 1)
    lo2 = jnp.where(lane < DECAY_LORA, jnp.tanh(lo2), lo2).astype(BF16)
    lo_out = jnp.dot(lo2, wlo_ref[...], preferred_element_type=F32) + bias_ref[...]
    gate = jnp.dot(jax.nn.sigmoid(xs[:, o5:]).astype(BF16), wg_ref[...], preferred_element_type=F32)
    dec_f, dec_b = lo_out[:, :o1], lo_out[:, o1:o2]
    icl_f, icl_b = lo_out[:, o2:o3], lo_out[:, o3:]
    m = 2.0 + (jax.nn.sigmoid(icl_f) + jax.nn.sigmoid(icl_b) - 2.0) * ka_ref[...]
    rkm = r * k * m
    seg = seg_ref[...]
    vk_o[0] = v * _segsum(rkm, seg)
    bonus_o[0] = v * _segsum(rkm * rk_ref[...], seg)
    gate_o[0] = gate
    for h in range(A_HEADS):
        sl = slice(h * A_HEAD_DIM, (h + 1) * A_HEAD_DIM)
        r_o[0, h] = r[:, sl]
        k_o[0, h] = k[:, sl]
        v_o[0, h] = v[:, sl]
        df_o[0, h] = dec_f[:, sl]
        db_o[0, h] = dec_b[:, sl]
        if_o[0, h] = icl_f[:, sl]
        ib_o[0, h] = icl_b[:, sl]


def rwkv_prep(p_a, n_ctx, tshift_mu, decay_w0, decay_up, icl_a0, icl_up, gate_up, k_a, r_k):
    bsz, t_all, _ = p_a.shape
    assert n_ctx % RW_TT == 0 and t_all % RW_TT == 0
    zeros = jnp.zeros((DECAY_LORA, 2 * A_WIDTH), F32)
    wlo = jnp.concatenate([
        jnp.concatenate([decay_up[0], decay_up[1], zeros], axis=1),
        jnp.concatenate([zeros, icl_up[0], icl_up[1]], axis=1)], axis=0).astype(BF16)
    bias = jnp.concatenate([decay_w0[0], decay_w0[1], icl_a0[0], icl_a0[1]])[None, :]
    nb8 = t_all // SUBLANES
    per8 = RW_TT // SUBLANES
    cm = jax.ShapeDtypeStruct((bsz, A_HEADS, t_all, A_HEAD_DIM), F32)
    tm = jax.ShapeDtypeStruct((bsz, t_all, A_WIDTH), F32)
    cm_spec = pl.BlockSpec((1, A_HEADS, RW_TT, A_HEAD_DIM), lambda b, j: (b, 0, j, 0))
    tm_spec = pl.BlockSpec((1, RW_TT, A_WIDTH), lambda b, j: (b, j, 0))
    return pl.pallas_call(
        functools.partial(_rwkv_prep_kernel, n_ctx=n_ctx, t_all=t_all),
        out_shape=[cm] * 7 + [tm] * 3,
        grid=(bsz, t_all // RW_TT),
        in_specs=[pl.BlockSpec((1, RW_TT, A_IN), lambda b, j: (b, j, 0)),
                  pl.BlockSpec((1, SUBLANES, A_IN), lambda b, j: (b, jnp.maximum(j * per8 - 1, 0), 0)),
                  pl.BlockSpec((1, SUBLANES, A_IN), lambda b, j: (b, jnp.minimum((j + 1) * per8, nb8 - 1), 0)),
                  _const_spec((2, A_IN), 2), _const_spec((2 * DECAY_LORA, 4 * A_WIDTH), 2),
                  _const_spec((GATE_LORA, A_WIDTH), 2), _const_spec((1, 4 * A_WIDTH), 2),
                  _const_spec((1, A_WIDTH), 2), _const_spec((1, A_WIDTH), 2), _const_spec((A_WIDTH, A_WIDTH), 2)],
        out_specs=[cm_spec] * 7 + [tm_spec] * 3,
        compiler_params=_params(2),
        name="rwkv_prep",
    )(p_a, p_a, p_a, tshift_mu, wlo, gate_up.astype(BF16), bias, k_a[None, :], r_k.reshape(1, A_WIDTH),
      _seg_matrix())


WKV_TB = 16
WKV_COLS = WKV_TB * A_HEAD_DIM


def _wkv_kernel(rf_ref, rb_ref, kf_ref, kb_ref, vf_ref, vb_ref, df_ref, db_ref, if_ref, ib_ref,
                kk_ref, ka_ref, yf_ref, yb_ref,
                s_ref, a_s, w_s, b_s, k_s, rp_s, v_s, y_s, *, n_chain):
    @pl.when(pl.program_id(0) == 0)
    def _():
        s_ref[...] = jnp.zeros_like(s_ref)

    lane = lax.broadcasted_iota(jnp.int32, (A_HEAD_DIM, LANES), 1)
    is_fwd = lane < n_chain
    pad = jnp.zeros((LANES - 2 * n_chain, WKV_COLS), F32)

    def to_lanes(f_ref, b_ref):
        x = jnp.concatenate([f_ref[...], b_ref[...], pad], axis=0)
        tiles = []
        for j in range(WKV_COLS // LANES):
            xt = x[:, j * LANES:(j + 1) * LANES].T
            tiles.append(xt.reshape(LANES // A_HEAD_DIM, A_HEAD_DIM, LANES))
        nat = jnp.concatenate(tiles, axis=0)
        return jnp.stack([jnp.where(is_fwd, nat[i], nat[WKV_TB - 1 - i]) for i in range(WKV_TB)], axis=0)

    r = to_lanes(rf_ref, rb_ref)
    k = to_lanes(kf_ref, kb_ref)
    v_s[...] = to_lanes(vf_ref, vb_ref)
    w = jnp.exp(-LOG_DECAY_SCALE * jax.nn.sigmoid(to_lanes(df_ref, db_ref)))
    icl = jax.nn.sigmoid(to_lanes(if_ref, ib_ref))
    kk = k * kk_ref[...]
    kk = kk * lax.rsqrt(jnp.sum(kk * kk, axis=1, keepdims=True) + 1e-12)
    b = kk * icl
    a_s[...] = -kk
    w_s[...] = w
    b_s[...] = b
    k_s[...] = k * (1.0 + (icl - 1.0) * ka_ref[...])
    rp_s[...] = w * r - kk * jnp.sum(b * r, axis=1, keepdims=True)

    def step(t, carry):
        a = a_s[t]
        w = w_s[t]
        b = b_s[t]
        k = k_s[t]
        rp = rp_s[t]
        for v in range(A_HEAD_DIM):
            s = s_ref[v]
            sa = jnp.sum(s * a, axis=0, keepdims=True)
            y = jnp.sum(s * rp, axis=0, keepdims=True)
            vv = v_s[t, v:v + 1, :]
            s_ref[v] = s * w + sa * b + vv * k
            y_s[t, v:v + 1, :] = y
        return carry

    lax.fori_loop(0, WKV_TB, step, 0)

    y = y_s[...]
    y = jnp.stack([jnp.where(is_fwd, y[i], y[WKV_TB - 1 - i]) for i in range(WKV_TB)], axis=0)
    y = y.reshape(WKV_COLS, LANES)
    for j in range(WKV_COLS // LANES):
        yt = y[j * LANES:(j + 1) * LANES, :].T
        yf_ref[:, j * LANES:(j + 1) * LANES] = yt[:n_chain]
        yb_ref[:, j * LANES:(j + 1) * LANES] = yt[n_chain:2 * n_chain]


def wkv_scan(r, k, v, dec_f, dec_b, icl_f, icl_b, k_k, k_a, n_ctx):
    bsz, h, t_all, kd = r.shape
    n_chain = bsz * h
    assert n_ctx % WKV_TB == 0 and t_all % WKV_TB == 0 and 2 * n_chain <= LANES
    nb_ctx = n_ctx // WKV_TB
    nb = t_all // WKV_TB

    def bwd_block(j):
        return jnp.where(j < nb_ctx, nb_ctx - 1 - j, nb + nb_ctx - 1 - j)

    two_d = lambda u: u.reshape(n_chain, t_all * kd)
    f_spec = pl.BlockSpec((n_chain, WKV_COLS), lambda j: (0, j))
    b_spec = pl.BlockSpec((n_chain, WKV_COLS), lambda j: (0, bwd_block(j)))
    tile_spec = pl.BlockSpec((kd, LANES), lambda j: (0, 0))

    def lane_tile(p):
        t = jnp.tile(p.reshape(h, kd).T, (1, 2 * bsz))
        return jnp.pad(t, ((0, 0), (0, LANES - 2 * n_chain)))

    seq = pltpu.VMEM((WKV_TB, kd, LANES), F32)
    out = jax.ShapeDtypeStruct((n_chain, t_all * kd), F32)
    yf, yb = pl.pallas_call(
        functools.partial(_wkv_kernel, n_chain=n_chain),
        out_shape=[out, out],
        grid=(nb,),
        in_specs=[f_spec, b_spec] * 5 + [tile_spec, tile_spec],
        out_specs=[f_spec, b_spec],
        scratch_shapes=[pltpu.VMEM((kd, kd, LANES), F32)] + [seq] * 7,
        compiler_params=_params(1),
        name="wkv_scan",
    )(two_d(r), two_d(r), two_d(k), two_d(k), two_d(v), two_d(v), two_d(dec_f), two_d(dec_b),
      two_d(icl_f), two_d(icl_b), lane_tile(k_k), lane_tile(k_a))
    return yf.reshape(r.shape), yb.reshape(r.shape)


def _rwkv_post_kernel(yf_ref, yb_ref, vk_ref, bonus_ref, gate_ref, g_ref, b_ref, seg_ref, o_ref):
    y = jnp.concatenate([yf_ref[0, h] + yb_ref[0, h] for h in range(A_HEADS)], axis=-1) + vk_ref[0]
    seg = seg_ref[...]
    mu = _segsum(y, seg) * (1.0 / A_HEAD_DIM)
    d = y - mu
    var = _segsum(d * d, seg) * (1.0 / A_HEAD_DIM)
    yn = d * lax.rsqrt(var + GN_EPS) * g_ref[...] + b_ref[...]
    o_ref[0] = ((yn + bonus_ref[0]) * gate_ref[0]).astype(o_ref.dtype)


def rwkv_post(yf, yb, vk, bonus, gate, lnx_g, lnx_b, out_dtype=BF16):
    bsz, h, t_all, kd = yf.shape
    cm_spec = pl.BlockSpec((1, h, RW_TT, kd), lambda b, j: (b, 0, j, 0))
    tm_spec = pl.BlockSpec((1, RW_TT, A_WIDTH), lambda b, j: (b, j, 0))
    return pl.pallas_call(
        _rwkv_post_kernel,
        out_shape=jax.ShapeDtypeStruct((bsz, t_all, A_WIDTH), out_dtype),
        grid=(bsz, t_all // RW_TT),
        in_specs=[cm_spec, cm_spec, tm_spec, tm_spec, tm_spec, _const_spec((1, A_WIDTH), 2),
                  _const_spec((1, A_WIDTH), 2), _const_spec((A_WIDTH, A_WIDTH), 2)],
        out_specs=tm_spec,
        compiler_params=_params(2),
        name="rwkv_post",
    )(yf, yb, vk, bonus, gate, lnx_g[None, :], lnx_b[None, :], _seg_matrix())


def rwkv_group(p_a, n_ctx, tshift_mu, decay_w0, decay_up, icl_a0, icl_up, gate_up, k_k, k_a, r_k, lnx_g, lnx_b,
               out_dtype=BF16):
    r, k, v, dec_f, dec_b, icl_f, icl_b, vk, bonus, gate = rwkv_prep(
        p_a, n_ctx, tshift_mu, decay_w0, decay_up, icl_a0, icl_up, gate_up, k_a, r_k)
    yf, yb = wkv_scan(r, k, v, dec_f, dec_b, icl_f, icl_b, k_k, k_a, n_ctx)
    return rwkv_post(yf, yb, vk, bonus, gate, lnx_g, lnx_b, out_dtype)


def _rope_lanes(z, cos, sin_lo, sin_hi):
    return (z * cos + pltpu.roll(z, LANES - SUBLANES, axis=1) * sin_lo + pltpu.roll(z, SUBLANES, axis=1) * sin_hi)


def _mla_q_kernel(pq_ref, gq_ref, w_ref, m_ref, gain_ref, cos_ref, slo_ref, shi_ref, q_o):
    h = _rms_rows(pq_ref[0], gq_ref[...]).astype(BF16)
    q = jnp.dot(h, w_ref[...], preferred_element_type=F32)
    ms = jnp.dot((q * q).astype(BF16), m_ref[...], preferred_element_type=F32)
    qh = q * lax.rsqrt(ms + EPS) * gain_ref[...]
    cos, slo, shi = cos_ref[...], slo_ref[...], shi_ref[...]
    tiles = [_rope_lanes(qh[:, i * HEAD_PAD:(i + 1) * HEAD_PAD], cos, slo, shi) for i in range(B_HEADS)]
    q_o[0] = (jnp.concatenate(tiles, axis=-1) * ATTN_SCALE).astype(BF16)


def _mla_kv_kernel(pkv_ref, gkv_ref, wk_ref, wv_ref, m_ref, gain_ref, gr_ref, cos_ref, slo_ref, shi_ref,
                   place_ref, k_o, v_o):
    x = pkv_ref[0]
    h = _rms_rows(x[:, :KV_LORA], gkv_ref[...]).astype(BF16)
    kn = jnp.dot(h, wk_ref[...], preferred_element_type=F32)
    v_o[0] = jnp.dot(h, wv_ref[...], preferred_element_type=F32).astype(BF16)
    ms = jnp.dot((kn * kn).astype(BF16), m_ref[...], preferred_element_type=F32)
    knh = kn * lax.rsqrt(ms + EPS) * gain_ref[...]
    kr = x[:, KV_LORA:]
    kr = kr * lax.rsqrt(jnp.sum(kr * kr, axis=-1, keepdims=True) * (1.0 / QK_ROPE) + EPS) * gr_ref[...]
    kr = _rope_lanes(kr, cos_ref[...], slo_ref[...], shi_ref[...]).astype(BF16)
    k_o[0] = (knh + jnp.dot(kr, place_ref[...], preferred_element_type=F32)).astype(BF16)


def _head_pattern(nope, rope):
    tile = jnp.concatenate([nope, rope, jnp.zeros((HEAD_PAD - QK_NOPE - QK_ROPE,), F32)])
    return jnp.tile(tile, B_HEADS)[None, :]


def _seg_mean_matrix(with_rope):
    i = jnp.arange(QP_WIDTH)
    head, off = i // HEAD_PAD, i % HEAD_PAD
    nope = off < QK_NOPE
    rope = jnp.logical_and(off >= QK_NOPE, off < QK_NOPE + QK_ROPE)
    same = head[:, None] == head[None, :]
    m = jnp.where(same & nope[:, None] & nope[None, :], 1.0 / QK_NOPE, 0.0)
    if with_rope:
        m = m + jnp.where(same & rope[:, None] & rope[None, :], 1.0 / QK_ROPE, 0.0)
    return m.astype(BF16)


def _rope_tables(n, n_ctx):
    t = jnp.arange(n, dtype=jnp.int32)
    r_pos = (t // GRID_W).astype(F32)
    c_pos = (t % GRID_W).astype(F32)
    axis_dim = QK_ROPE // 2
    inv_freq = 1.0 / (ROPE_BASE ** (jnp.arange(0, axis_dim, 2, dtype=F32) / axis_dim))
    ang_r = r_pos[:, None] * inv_freq
    ang_c = c_pos[:, None] * inv_freq
    half = axis_dim // 2
    z = jnp.zeros((n, half), F32)
    cos = jnp.concatenate([jnp.cos(ang_r)] * 2 + [jnp.cos(ang_c)] * 2, axis=1)
    s_lo = jnp.concatenate([-jnp.sin(ang_r), z, -jnp.sin(ang_c), z], axis=1)
    s_hi = jnp.concatenate([z, jnp.sin(ang_r), z, jnp.sin(ang_c)], axis=1)

    def widen(tab, fill):
        tab = jnp.concatenate([jnp.full((n_ctx, QK_ROPE), fill, F32), tab], axis=0)
        return jnp.concatenate([jnp.full((n_ctx + n, QK_NOPE), fill, F32), tab,
                                jnp.full((n_ctx + n, HEAD_PAD - QK_NOPE - QK_ROPE), fill, F32)], axis=1)

    return widen(cos, 1.0), widen(s_lo, 0.0), widen(s_hi, 0.0)


def mla_qkv(p_q, p_kv, tabs, q_norm_g, kv_norm_g, w_uq, w_ukv, q_nope_g, k_nope_g, q_rope_g, k_rope_g):
    bsz, t_all, _ = p_q.shape
    nblk = t_all // BM
    dq = QK_NOPE + QK_ROPE
    w_q = jnp.pad(w_uq.reshape(Q_LORA, B_HEADS, dq), ((0, 0), (0, 0), (0, HEAD_PAD - dq))).reshape(
        Q_LORA, QP_WIDTH).astype(BF16)
    w4 = w_ukv.reshape(KV_LORA, B_HEADS, QK_NOPE + V_HEAD)
    w_k = jnp.pad(w4[:, :, :QK_NOPE], ((0, 0), (0, 0), (0, HEAD_PAD - QK_NOPE))).reshape(
        KV_LORA, QP_WIDTH).astype(BF16)
    w_v = w4[:, :, QK_NOPE:].reshape(KV_LORA, B_WIDTH).astype(BF16)
    zero_r = jnp.zeros((QK_ROPE,), F32)
    i = jnp.arange(LANES)
    rope_lane = jnp.logical_and(i >= QK_NOPE, i < QK_NOPE + QK_ROPE)
    place = jnp.where(rope_lane[:, None] & (i[:, None] == (jnp.arange(QP_WIDTH) % HEAD_PAD)[None, :]),
                      1.0, 0.0).astype(BF16)
    gr = jnp.concatenate([jnp.zeros((QK_NOPE,), F32), k_rope_g, jnp.zeros((LANES - QK_NOPE - QK_ROPE,), F32)])

    row_spec = lambda n: pl.BlockSpec((1, BM, n), lambda j, b: (b, j, 0))
    tab_spec = pl.BlockSpec((BM, LANES), lambda j, b: (j, 0))
    c = lambda shape: _const_spec(shape, 2)
    q = pl.pallas_call(
        _mla_q_kernel,
        out_shape=jax.ShapeDtypeStruct((bsz, t_all, QP_WIDTH), BF16),
        grid=(nblk, bsz),
        in_specs=[row_spec(Q_LORA), c((1, Q_LORA)), c((Q_LORA, QP_WIDTH)), c((QP_WIDTH, QP_WIDTH)),
                  c((1, QP_WIDTH)), tab_spec, tab_spec, tab_spec],
        out_specs=row_spec(QP_WIDTH),
        compiler_params=_params(2),
        name="mla_q",
    )(p_q, q_norm_g[None, :], w_q, _seg_mean_matrix(True), _head_pattern(q_nope_g, q_rope_g), *tabs)
    k, v = pl.pallas_call(
        _mla_kv_kernel,
        out_shape=[jax.ShapeDtypeStruct((bsz, t_all, QP_WIDTH), BF16),
                   jax.ShapeDtypeStruct((bsz, t_all, B_WIDTH), BF16)],
        grid=(nblk, bsz),
        in_specs=[row_spec(KV_IN_PAD), c((1, KV_LORA)), c((KV_LORA, QP_WIDTH)), c((KV_LORA, B_WIDTH)),
                  c((QP_WIDTH, QP_WIDTH)), c((1, QP_WIDTH)), c((1, LANES)), tab_spec, tab_spec, tab_spec,
                  c((LANES, QP_WIDTH))],
        out_specs=[row_spec(QP_WIDTH), row_spec(B_WIDTH)],
        compiler_params=_params(2),
        name="mla_kv",
    )(p_kv, kv_norm_g[None, :], w_k, w_v, _seg_mean_matrix(False), _head_pattern(k_nope_g, zero_r),
      gr[None, :], *tabs, place)
    return q, k, v


ATT_BQ = 256
HEADS_PER_STEP = 2


def _attn_kernel(q_ref, k_ref, v_ref, o_ref):
    v = v_ref[0]
    lane = lax.broadcasted_iota(jnp.int32, o_ref.shape[1:], 1)
    out = None
    for h in range(HEADS_PER_STEP):
        q = q_ref[0, :, h * HEAD_PAD:(h + 1) * HEAD_PAD]
        k = k_ref[0, :, h * HEAD_PAD:(h + 1) * HEAD_PAD]
        s = lax.dot_general(q, k, (((1,), (1,)), ((), ())), preferred_element_type=F32)
        p = jnp.exp(s - jnp.max(s, axis=-1, keepdims=True))
        l = jnp.sum(p, axis=-1, keepdims=True)
        o = jnp.dot(p.astype(BF16), v, preferred_element_type=F32) / l
        out = o if out is None else jnp.where(lane < h * V_HEAD, out, o)
    o_ref[0] = out.astype(o_ref.dtype)


def attention(q, k, v, q_row0, nq, nk):
    bsz = q.shape[0]
    bq = min(ATT_BQ, nq)
    qb0 = q_row0 // bq
    pair = HEADS_PER_STEP * HEAD_PAD
    return pl.pallas_call(
        _attn_kernel,
        out_shape=jax.ShapeDtypeStruct((bsz, nq, B_WIDTH), BF16),
        grid=(bsz, B_HEADS // HEADS_PER_STEP, nq // bq),
        in_specs=[pl.BlockSpec((1, bq, pair), lambda b, hp, i: (b, qb0 + i, hp)),
                  pl.BlockSpec((1, nk, pair), lambda b, hp, i: (b, 0, hp)),
                  pl.BlockSpec((1, nk, LANES), lambda b, hp, i: (b, 0, hp))],
        out_specs=pl.BlockSpec((1, bq, LANES), lambda b, hp, i: (b, i, hp)),
        compiler_params=_params(3),
        name="mla_attention",
    )(q, k, v)


def _outproj_kernel(ya_ref, ybc_ref, ybl_ref, pc_ref, pcp_ref, pcn_ref, x_ref, mod_ref, cw_ref, w_ref, o_ref, *,
                    row_off, n_ctx, t_all):
    j = pl.program_id(1) + row_off
    t0 = j * BM
    first = jnp.logical_or(t0 == 0, t0 == n_ctx)
    last = jnp.logical_or(t0 + BM == n_ctx, t0 + BM == t_all)
    o1, o2 = C_WIDTH, 2 * C_WIDTH
    pc = pc_ref[0]
    u = pc[:, o1:o2] * pc[:, o2:]
    pcp = pcp_ref[0, SUBLANES - 1:SUBLANES, :]
    pcn = pcn_ref[0, 0:1, :]
    u_prev = jnp.where(first, 0.0, pcp[:, o1:o2] * pcp[:, o2:])
    u_next = jnp.where(last, 0.0, pcn[:, o1:o2] * pcn[:, o2:])
    rows = lax.broadcasted_iota(jnp.int32, u.shape, 0)
    up = jnp.where(rows == 0, u_prev, pltpu.roll(u, 1, axis=0))
    un = jnp.where(rows == BM - 1, u_next, pltpu.roll(u, BM - 1, axis=0))
    yc = pc[:, :o1] * (cw_ref[0:1, :] * up + cw_ref[1:2, :] * u + cw_ref[2:3, :] * un)
    yb = jnp.where(j == 0, ybc_ref[0], ybl_ref[0]) if row_off == 0 else ybl_ref[0]
    acc = jnp.dot(ya_ref[0], w_ref[0:A_WIDTH, :], preferred_element_type=F32)
    acc += jnp.dot(yb, w_ref[A_WIDTH:A_WIDTH + B_WIDTH, :], preferred_element_type=F32)
    acc += jnp.dot(yc.astype(BF16), w_ref[A_WIDTH + B_WIDTH:, :], preferred_element_type=F32)
    o_ref[0] = x_ref[0] + mod_ref[0, :, 2 * D_MODEL:3 * D_MODEL] * acc


def outproj(ya, yb_ctx, yb_lat, p_c, xa, mod, conv_w, w_out, n_ctx, with_ctx):
    bsz, t_all, d = xa.shape
    row_off = 0 if with_ctx else n_ctx // BM
    nblk = t_all // BM - row_off
    nb8 = t_all // SUBLANES
    per8 = BM // SUBLANES
    rows = lambda n: pl.BlockSpec((1, BM, n), lambda b, j: (b, j + row_off, 0))
    lat_off = n_ctx // BM
    return pl.pallas_call(
        functools.partial(_outproj_kernel, row_off=row_off, n_ctx=n_ctx, t_all=t_all),
        out_shape=jax.ShapeDtypeStruct((bsz, nblk * BM, d), F32),
        grid=(bsz, nblk),
        in_specs=[rows(A_WIDTH),
                  pl.BlockSpec((1, BM, B_WIDTH), lambda b, j: (b, 0, 0)),
                  pl.BlockSpec((1, BM, B_WIDTH), lambda b, j: (b, jnp.maximum(j + row_off - lat_off, 0), 0)),
                  rows(C_IN),
                  pl.BlockSpec((1, SUBLANES, C_IN), lambda b, j: (b, jnp.maximum((j + row_off) * per8 - 1, 0), 0)),
                  pl.BlockSpec((1, SUBLANES, C_IN),
                               lambda b, j: (b, jnp.minimum((j + row_off + 1) * per8, nb8 - 1), 0)),
                  rows(d),
                  pl.BlockSpec((1, 1, 6 * d), lambda b, j: (2 * b + jnp.minimum(j + row_off, 1), 0, 0)),
                  _const_spec((3, C_WIDTH), 2), _const_spec((d, d), 2)],
        out_specs=pl.BlockSpec((1, BM, d), lambda b, j: (b, j, 0)),
        compiler_params=_params(2),
        name="outproj",
    )(ya, yb_ctx, yb_lat, p_c, p_c, p_c, xa, mod, conv_w, w_out)


FFN_CHUNKS = 2


def _ffn_in_kernel(x_ref, mod_ref, g_ref, w_ref, o_ref):
    sh = mod_ref[0, :, 3 * D_MODEL:4 * D_MODEL]
    sc = mod_ref[0, :, 4 * D_MODEL:5 * D_MODEL]
    h = (_rms_rows(x_ref[0], g_ref[...]) * (1.0 + sc) + sh).astype(BF16)
    d_ff = o_ref.shape[-1]
    cw = d_ff // FFN_CHUNKS
    for c in range(FFN_CHUNKS):
        gate = jnp.dot(h, w_ref[:, c * cw:(c + 1) * cw], preferred_element_type=F32)
        up = jnp.dot(h, w_ref[:, d_ff + c * cw:d_ff + (c + 1) * cw], preferred_element_type=F32)
        o_ref[0, :, c * cw:(c + 1) * cw] = (gate * jax.nn.sigmoid(gate) * up).astype(o_ref.dtype)


def _ffn_out_kernel(a_ref, x_ref, mod_ref, w_ref, o_ref):
    acc = jnp.dot(a_ref[0], w_ref[...], preferred_element_type=F32)
    o_ref[0] = x_ref[0] + mod_ref[0, :, 5 * D_MODEL:6 * D_MODEL] * acc


def ffn(x1, mod, g, w_fi, w_fo, lat_only):
    bsz, t, d = x1.shape
    d_ff = w_fo.shape[0]
    seg = 1 if lat_only else 0
    rows = lambda n: pl.BlockSpec((1, BM, n), lambda b, j: (b, j, 0))
    mod_spec = pl.BlockSpec((1, 1, 6 * d), lambda b, j: (2 * b + jnp.minimum(j + seg, 1), 0, 0))
    act = pl.pallas_call(
        _ffn_in_kernel,
        out_shape=jax.ShapeDtypeStruct((bsz, t, d_ff), BF16),
        grid=(bsz, t // BM),
        in_specs=[rows(d), mod_spec, _const_spec((1, d), 2), _const_spec((d, 2 * d_ff), 2)],
        out_specs=rows(d_ff),
        compiler_params=_params(2),
        name="ffn_in",
    )(x1, mod, g[None, :], w_fi)
    return pl.pallas_call(
        _ffn_out_kernel,
        out_shape=jax.ShapeDtypeStruct((bsz, t, d), F32),
        grid=(bsz, t // BM),
        in_specs=[rows(d_ff), rows(d), mod_spec, _const_spec((d_ff, d), 2)],
        out_specs=rows(d),
        compiler_params=_params(2),
        name="ffn_out",
    )(act, x1, mod, w_fo)


def kernel(x, c, ctx, c_ctx, ada_w, ada_b, norm1_g, norm2_g, w_in, tshift_mu, decay_w0, decay_up, icl_a0, icl_up, gate_up, k_k, k_a, r_k, lnx_g, lnx_b, q_norm_g, kv_norm_g, w_uq, w_ukv, q_nope_g, k_nope_g, q_rope_g, k_rope_g, conv_w, w_out, w_ffn_in, w_ffn_out):
    bsz, n, d = x.shape
    n_ctx = ctx.shape[1]
    depth = ada_w.shape[0]
    assert n_ctx == BM and n % BM == 0 and d == D_MODEL
    tabs = _rope_tables(n, n_ctx)

    xa = jnp.concatenate([ctx, x], axis=1)
    cond = jnp.concatenate([c, c_ctx[None, :], jnp.zeros((2 * SUBLANES - bsz - 1, d), c.dtype)], axis=0)
    silu_cond = cond * jax.nn.sigmoid(cond)
    o_kv = A_IN + Q_LORA + KV_LORA
    for l in range(depth):
        last = l == depth - 1
        ada = pmm(silu_cond, ada_w[l]) + ada_b[l]
        mod = jnp.stack([jnp.broadcast_to(ada[bsz], (bsz, 6 * d)), ada[:bsz]], axis=1).reshape(2 * bsz, 1, 6 * d)

        wl = w_in[l]
        w_p = jnp.concatenate(
            [wl[:, :o_kv], jnp.zeros((d, QK_NOPE), F32), wl[:, o_kv:A_IN + B_IN],
             jnp.zeros((d, LANES - QK_NOPE - QK_ROPE), F32), wl[:, A_IN + B_IN:]], axis=1).astype(BF16)
        p_a, p_q, p_kv, p_c = inproj(xa, mod, norm1_g[l], w_p)

        ya = rwkv_group(p_a, n_ctx, tshift_mu[l], decay_w0[l], decay_up[l], icl_a0[l], icl_up[l], gate_up[l],
                        k_k[l], k_a[l], r_k[l], lnx_g[l], lnx_b[l])
        q, k, v = mla_qkv(p_q, p_kv, tabs, q_norm_g[l], kv_norm_g[l], w_uq[l], w_ukv[l], q_nope_g[l],
                          k_nope_g[l], q_rope_g[l], k_rope_g[l])
        yb_lat = attention(q, k, v, n_ctx, n, n_ctx + n)
        yb_ctx = yb_lat if last else attention(q, k, v, 0, n_ctx, n_ctx)

        x1 = outproj(ya, yb_ctx, yb_lat, p_c, xa, mod, conv_w[l], w_out[l].astype(BF16), n_ctx, not last)
        xa = ffn(x1, mod, norm2_g[l], w_ffn_in[l].astype(BF16), w_ffn_out[l].astype(BF16), last)
    return xa
```

```python
import functools

import jax
import jax.numpy as jnp
from jax import lax
from jax.experimental import pallas as pl
from jax.experimental.pallas import tpu as pltpu

D_MODEL = 1024
GRID_W = 64
A_HEADS = 6
A_HEAD_DIM = 64
A_WIDTH = A_HEADS * A_HEAD_DIM
DECAY_LORA = 64
ICL_LORA = 64
GATE_LORA = 128
LOG_DECAY_SCALE = 0.606531
GN_EPS = A_HEAD_DIM * 1e-5
B_HEADS = 6
QK_NOPE = 64
QK_ROPE = 32
V_HEAD = 64
B_WIDTH = B_HEADS * V_HEAD
Q_LORA = 768
KV_LORA = 256
ROPE_BASE = 10000.0
ATTN_SCALE = (QK_NOPE + QK_ROPE) ** -0.5
C_WIDTH = D_MODEL - A_WIDTH - B_WIDTH
A_IN = 3 * A_WIDTH + DECAY_LORA + ICL_LORA + GATE_LORA
B_IN = Q_LORA + KV_LORA + QK_ROPE
C_IN = 3 * C_WIDTH
EPS = 1e-6

LANES = 128
SUBLANES = 8
HEAD_PAD = LANES
QP_WIDTH = B_HEADS * HEAD_PAD
KV_IN_PAD = KV_LORA + LANES
P_PAD = A_IN + Q_LORA + KV_IN_PAD + C_IN
VMEM_LIMIT = 48 * 1024 * 1024
BM = 256
F32 = jnp.float32
BF16 = jnp.bfloat16


def _const_spec(shape, nargs):
    zeros = tuple(0 for _ in shape)
    if nargs == 1:
        return pl.BlockSpec(shape, lambda i: zeros)
    if nargs == 2:
        return pl.BlockSpec(shape, lambda i, j: zeros)
    return pl.BlockSpec(shape, lambda i, j, k: zeros)


def _params(n_axes):
    return pltpu.CompilerParams(dimension_semantics=("arbitrary",) * n_axes, vmem_limit_bytes=VMEM_LIMIT)


def _rms_rows(x, g):
    return x * lax.rsqrt(jnp.mean(x * x, axis=-1, keepdims=True) + EPS) * g


def _mm_kernel(x_ref, w_ref, o_ref):
    o_ref[...] = jnp.dot(x_ref[...].astype(BF16), w_ref[...].astype(BF16), preferred_element_type=F32)


def pmm(x, w, bn=1024):
    m, k = x.shape
    n = w.shape[1]
    return pl.pallas_call(
        _mm_kernel,
        out_shape=jax.ShapeDtypeStruct((m, n), F32),
        grid=(n // bn,),
        in_specs=[pl.BlockSpec((m, k), lambda j: (0, 0)), pl.BlockSpec((k, bn), lambda j: (0, j))],
        out_specs=pl.BlockSpec((m, bn), lambda j: (0, j)),
        compiler_params=_params(1),
        name="pmm",
    )(x, w)


def _inproj_kernel(x_ref, mod_ref, g_ref, w_ref, pa_o, pq_o, pkv_o, pc_o):
    sh = mod_ref[0, :, 0:D_MODEL]
    sc = mod_ref[0, :, D_MODEL:2 * D_MODEL]
    h = (_rms_rows(x_ref[0], g_ref[...]) * (1.0 + sc) + sh).astype(BF16)
    o = 0
    for out in (pa_o, pq_o, pkv_o, pc_o):
        n = out.shape[-1]
        out[0] = jnp.dot(h, w_ref[:, o:o + n], preferred_element_type=F32)
        o += n


def inproj(xa, mod, g, w_p):
    bsz, t_all, d = xa.shape
    nblk = t_all // BM
    widths = (A_IN, Q_LORA, KV_IN_PAD, C_IN)
    return pl.pallas_call(
        _inproj_kernel,
        out_shape=[jax.ShapeDtypeStruct((bsz, t_all, n), F32) for n in widths],
        grid=(bsz, nblk),
        in_specs=[pl.BlockSpec((1, BM, d), lambda b, j: (b, j, 0)),
                  pl.BlockSpec((1, 1, 6 * d), lambda b, j: (2 * b + jnp.minimum(j, 1), 0, 0)),
                  _const_spec((1, d), 2), _const_spec((d, P_PAD), 2)],
        out_specs=[pl.BlockSpec((1, BM, n), lambda b, j: (b, j, 0)) for n in widths],
        compiler_params=_params(2),
        name="inproj",
    )(xa, mod, g[None, :], w_p)


def _seg_matrix():
    h = jnp.arange(A_WIDTH) // A_HEAD_DIM
    return (h[:, None] == h[None, :]).astype(BF16)


def _segsum(z, seg):
    hi = z.astype(BF16)
    lo = (z - hi.astype(F32)).astype(BF16)
    return jnp.dot(hi, seg, preferred_element_type=F32) + jnp.dot(lo, seg, preferred_element_type=F32)


RW_TT = 128
CHAIN_PAD = SUBLANES


def _rwkv_prep_kernel(p_ref, pprev_ref, pnext_ref, mu_ref, wlo_ref, wg_ref, bias_ref, ka_ref, rk_ref, seg_ref,
                      r_o, k_o, v_o, df_o, db_o, if_o, ib_o, vk_o, bonus_o, gate_o, pk_s, *, n_ctx, t_all):
    t0 = pl.program_id(1) * RW_TT
    x = p_ref[0]
    first = jnp.logical_or(t0 == 0, t0 == n_ctx)
    last = jnp.logical_or(t0 + RW_TT == n_ctx, t0 + RW_TT == t_all)
    prev_row = jnp.where(first, 0.0, pprev_ref[0, SUBLANES - 1:SUBLANES, :])
    next_row = jnp.where(last, 0.0, pnext_ref[0, 0:1, :])
    rows = lax.broadcasted_iota(jnp.int32, x.shape, 0)
    xp = jnp.where(rows == 0, prev_row, pltpu.roll(x, 1, axis=0))
    xn = jnp.where(rows == RW_TT - 1, next_row, pltpu.roll(x, RW_TT - 1, axis=0))
    xs = x + mu_ref[0:1, :] * (xp - x) + mu_ref[1:2, :] * (xn - x)

    o1, o2, o3 = A_WIDTH, 2 * A_WIDTH, 3 * A_WIDTH
    o5 = o3 + DECAY_LORA + ICL_LORA
    r, k, v = xs[:, :o1], xs[:, o1:o2], xs[:, o2:o3]
    lo2 = xs[:, o3:o5]
    lane = lax.broadcasted_iota(jnp.int32, lo2.shape, 1)
    lo2 = jnp.where(lane < DECAY_LORA, jnp.tanh(lo2), lo2).astype(BF16)
    lo_out = jnp.dot(lo2, wlo_ref[...], preferred_element_type=F32) + bias_ref[...]
    gate = jnp.dot(jax.nn.sigmoid(xs[:, o5:]).astype(BF16), wg_ref[...], preferred_element_type=F32)
    dec_f, dec_b = lo_out[:, :o1], lo_out[:, o1:o2]
    icl_f, icl_b = lo_out[:, o2:o3], lo_out[:, o3:]
    m = 2.0 + (jax.nn.sigmoid(icl_f) + jax.nn.sigmoid(icl_b) - 2.0) * ka_ref[...]
    rkm = r * k * m
    seg = seg_ref[...]
    vk_o[0] = v * _segsum(rkm, seg)
    bonus_o[0] = v * _segsum(rkm * rk_ref[...], seg)
    gate_o[0] = gate
    half = RW_TT // 2
    zero = jnp.zeros((half, LANES), F32)
    for i, (val, out) in enumerate(((r, r_o), (k, k_o), (v, v_o), (dec_f, df_o), (dec_b, db_o),
                                    (icl_f, if_o), (icl_b, ib_o))):
        heads = []
        for cb in range(A_WIDTH // LANES):
            pk_s[i, cb] = val[:, cb * LANES:(cb + 1) * LANES]
            even = pk_s[i, cb, pl.ds(0, half, stride=2), :]
            odd = pk_s[i, cb, pl.ds(1, half, stride=2), :]
            for hh in range(2):
                sl = slice(hh * A_HEAD_DIM, (hh + 1) * A_HEAD_DIM)
                heads.append(jnp.concatenate([even[:, sl], odd[:, sl]], axis=-1))
        heads += [zero] * (CHAIN_PAD - A_HEADS)
        out[...] = jnp.swapaxes(jnp.stack(heads, axis=0), 0, 1)


def rwkv_prep(p_a, n_ctx, tshift_mu, decay_w0, decay_up, icl_a0, icl_up, gate_up, k_a, r_k):
    bsz, t_all, _ = p_a.shape
    assert n_ctx % RW_TT == 0 and t_all % RW_TT == 0
    zeros = jnp.zeros((DECAY_LORA, 2 * A_WIDTH), F32)
    wlo = jnp.concatenate([
        jnp.concatenate([decay_up[0], decay_up[1], zeros], axis=1),
        jnp.concatenate([zeros, icl_up[0], icl_up[1]], axis=1)], axis=0).astype(BF16)
    bias = jnp.concatenate([decay_w0[0], decay_w0[1], icl_a0[0], icl_a0[1]])[None, :]
    nb8 = t_all // SUBLANES
    per8 = RW_TT // SUBLANES
    cm = jax.ShapeDtypeStruct((t_all // 2, bsz * CHAIN_PAD, LANES), F32)
    tm = jax.ShapeDtypeStruct((bsz, t_all, A_WIDTH), F32)
    cm_spec = pl.BlockSpec((RW_TT // 2, CHAIN_PAD, LANES), lambda b, j: (j, b, 0))
    tm_spec = pl.BlockSpec((1, RW_TT, A_WIDTH), lambda b, j: (b, j, 0))
    return pl.pallas_call(
        functools.partial(_rwkv_prep_kernel, n_ctx=n_ctx, t_all=t_all),
        out_shape=[cm] * 7 + [tm] * 3,
        grid=(bsz, t_all // RW_TT),
        in_specs=[pl.BlockSpec((1, RW_TT, A_IN), lambda b, j: (b, j, 0)),
                  pl.BlockSpec((1, SUBLANES, A_IN), lambda b, j: (b, jnp.maximum(j * per8 - 1, 0), 0)),
                  pl.BlockSpec((1, SUBLANES, A_IN), lambda b, j: (b, jnp.minimum((j + 1) * per8, nb8 - 1), 0)),
                  _const_spec((2, A_IN), 2), _const_spec((2 * DECAY_LORA, 4 * A_WIDTH), 2),
                  _const_spec((GATE_LORA, A_WIDTH), 2), _const_spec((1, 4 * A_WIDTH), 2),
                  _const_spec((1, A_WIDTH), 2), _const_spec((1, A_WIDTH), 2), _const_spec((A_WIDTH, A_WIDTH), 2)],
        out_specs=[cm_spec] * 7 + [tm_spec] * 3,
        scratch_shapes=[pltpu.VMEM((7, A_WIDTH // LANES, RW_TT, LANES), F32)],
        compiler_params=_params(2),
        name="rwkv_prep",
    )(p_a, p_a, p_a, tshift_mu, wlo, gate_up.astype(BF16), bias, k_a[None, :], r_k.reshape(1, A_WIDTH),
      _seg_matrix())


WKV_TB = 16
WKV_ROWS = WKV_TB // 2
WKV_ACC = 4


def _wkv_kernel(rf_ref, rb_ref, kf_ref, kb_ref, vf_ref, vb_ref, df_ref, db_ref, if_ref, ib_ref,
                kk_ref, ka_ref, yf_ref, yb_ref,
                s_ref, a_s, b_s, k_s, rp_s, v_s, y_s, g_s, nat_s, *, n_chain):
    @pl.when(pl.program_id(0) == 0)
    def _():
        s_ref[...] = jnp.zeros_like(s_ref)

    lane = lax.broadcasted_iota(jnp.int32, (A_HEAD_DIM, LANES), 1)
    is_fwd = lane < n_chain

    for q, (f_ref, b_ref) in enumerate(((rf_ref, rb_ref), (kf_ref, kb_ref), (vf_ref, vb_ref),
                                        (df_ref, db_ref), (if_ref, ib_ref))):
        for j in range(WKV_ROWS):
            x = jnp.concatenate([f_ref[j], b_ref[j]], axis=0)
            nat_s[q, 2 * j:2 * j + 2] = x.T.reshape(2, A_HEAD_DIM, LANES)

    kk_gain = kk_ref[...]
    ka = ka_ref[...]
    g_prev = None
    for i in range(WKV_TB):
        r, k, v, dec, icl = (jnp.where(is_fwd, nat_s[q, i], nat_s[q, WKV_TB - 1 - i]) for q in range(5))
        w = jnp.exp(-LOG_DECAY_SCALE * jax.nn.sigmoid(dec))
        icl = jax.nn.sigmoid(icl)
        g = w if g_prev is None else g_prev * w
        g_inv = 1.0 / g
        kk = k * kk_gain
        kk = kk * lax.rsqrt(jnp.sum(kk * kk, axis=0, keepdims=True) + 1e-12)
        b = kk * icl
        rp = w * r - kk * jnp.sum(b * r, axis=0, keepdims=True)
        v_s[i] = v
        a_s[i] = -kk if g_prev is None else -kk * g_prev
        rp_s[i] = rp if g_prev is None else rp * g_prev
        b_s[i] = b * g_inv
        k_s[i] = k * (1.0 + (icl - 1.0) * ka) * g_inv
        g_prev = g
    g_s[...] = g_prev

    def step(t, carry):
        sa_p = [None] * WKV_ACC
        y_p = [None] * WKV_ACC
        for kx in range(A_HEAD_DIM):
            s = s_ref[kx]
            pa = s * a_s[t, kx:kx + 1, :]
            py = s * rp_s[t, kx:kx + 1, :]
            i = kx % WKV_ACC
            sa_p[i] = pa if sa_p[i] is None else sa_p[i] + pa
            y_p[i] = py if y_p[i] is None else y_p[i] + py
        sa = functools.reduce(lambda u, w: u + w, sa_p)
        y_s[t] = functools.reduce(lambda u, w: u + w, y_p)
        vt = v_s[t]
        for kx in range(A_HEAD_DIM):
            s_ref[kx] = s_ref[kx] + sa * b_s[t, kx:kx + 1, :] + vt * k_s[t, kx:kx + 1, :]
        return carry

    lax.fori_loop(0, WKV_TB, step, 0)

    for kx in range(A_HEAD_DIM):
        s_ref[kx] = s_ref[kx] * g_s[kx:kx + 1, :]

    for j in range(WKV_ROWS):
        y = jnp.concatenate([jnp.where(is_fwd, y_s[i], y_s[WKV_TB - 1 - i]) for i in (2 * j, 2 * j + 1)], axis=0)
        yt = y.T
        yf_ref[j] = yt[:n_chain]
        yb_ref[j] = yt[n_chain:]


def wkv_scan(r, k, v, dec_f, dec_b, icl_f, icl_b, k_k, k_a, n_ctx):
    t_half, n_chain, _ = r.shape
    t_all, kd = 2 * t_half, A_HEAD_DIM
    assert n_ctx % WKV_TB == 0 and t_all % WKV_TB == 0 and 2 * n_chain == LANES
    nb_ctx = n_ctx // WKV_TB
    nb = t_all // WKV_TB

    def bwd_block(j):
        return jnp.where(j < nb_ctx, nb_ctx - 1 - j, nb + nb_ctx - 1 - j)

    f_spec = pl.BlockSpec((WKV_ROWS, n_chain, LANES), lambda j: (j, 0, 0))
    b_spec = pl.BlockSpec((WKV_ROWS, n_chain, LANES), lambda j: (bwd_block(j), 0, 0))
    tile_spec = pl.BlockSpec((kd, LANES), lambda j: (0, 0))

    def lane_tile(p):
        t = jnp.pad(p.reshape(A_HEADS, kd).T, ((0, 0), (0, CHAIN_PAD - A_HEADS)))
        return jnp.tile(t, (1, LANES // CHAIN_PAD))

    seq = pltpu.VMEM((WKV_TB, kd, LANES), F32)
    out = jax.ShapeDtypeStruct((t_half, n_chain, LANES), F32)
    yf, yb = pl.pallas_call(
        functools.partial(_wkv_kernel, n_chain=n_chain),
        out_shape=[out, out],
        grid=(nb,),
        in_specs=[f_spec, b_spec] * 5 + [tile_spec, tile_spec],
        out_specs=[f_spec, b_spec],
        scratch_shapes=[pltpu.VMEM((kd, kd, LANES), F32)] + [seq] * 6 + [
            pltpu.VMEM((kd, LANES), F32), pltpu.VMEM((5, WKV_TB, kd, LANES), F32)],
        compiler_params=_params(1),
        name="wkv_scan",
    )(r, r, k, k, v, v, dec_f, dec_b, icl_f, icl_b, lane_tile(k_k), lane_tile(k_a))
    return yf, yb


def _rwkv_post_kernel(yf_ref, yb_ref, vk_ref, bonus_ref, gate_ref, g_ref, b_ref, seg_ref, o_ref, y_s):
    half = RW_TT // 2
    yh = jnp.swapaxes(yf_ref[...] + yb_ref[...], 0, 1)
    for cb in range(A_WIDTH // LANES):
        y0 = yh[2 * cb]
        y1 = yh[2 * cb + 1]
        y_s[cb, pl.ds(0, half, stride=2), :] = jnp.concatenate([y0[:, :A_HEAD_DIM], y1[:, :A_HEAD_DIM]], axis=-1)
        y_s[cb, pl.ds(1, half, stride=2), :] = jnp.concatenate([y0[:, A_HEAD_DIM:], y1[:, A_HEAD_DIM:]], axis=-1)
    y = jnp.concatenate([y_s[cb] for cb in range(A_WIDTH // LANES)], axis=-1) + vk_ref[0]
    seg = seg_ref[...]
    mu = _segsum(y, seg) * (1.0 / A_HEAD_DIM)
    d = y - mu
    var = _segsum(d * d, seg) * (1.0 / A_HEAD_DIM)
    yn = d * lax.rsqrt(var + GN_EPS) * g_ref[...] + b_ref[...]
    o_ref[0] = ((yn + bonus_ref[0]) * gate_ref[0]).astype(o_ref.dtype)


def rwkv_post(yf, yb, vk, bonus, gate, lnx_g, lnx_b, out_dtype=BF16):
    bsz, t_all, _ = vk.shape
    cm_spec = pl.BlockSpec((RW_TT // 2, CHAIN_PAD, LANES), lambda b, j: (j, b, 0))
    tm_spec = pl.BlockSpec((1, RW_TT, A_WIDTH), lambda b, j: (b, j, 0))
    return pl.pallas_call(
        _rwkv_post_kernel,
        out_shape=jax.ShapeDtypeStruct((bsz, t_all, A_WIDTH), out_dtype),
        grid=(bsz, t_all // RW_TT),
        in_specs=[cm_spec, cm_spec, tm_spec, tm_spec, tm_spec, _const_spec((1, A_WIDTH), 2),
                  _const_spec((1, A_WIDTH), 2), _const_spec((A_WIDTH, A_WIDTH), 2)],
        out_specs=tm_spec,
        scratch_shapes=[pltpu.VMEM((A_WIDTH // LANES, RW_TT, LANES), F32)],
        compiler_params=_params(2),
        name="rwkv_post",
    )(yf, yb, vk, bonus, gate, lnx_g[None, :], lnx_b[None, :], _seg_matrix())


def rwkv_group(p_a, n_ctx, tshift_mu, decay_w0, decay_up, icl_a0, icl_up, gate_up, k_k, k_a, r_k, lnx_g, lnx_b,
               out_dtype=BF16):
    r, k, v, dec_f, dec_b, icl_f, icl_b, vk, bonus, gate = rwkv_prep(
        p_a, n_ctx, tshift_mu, decay_w0, decay_up, icl_a0, icl_up, gate_up, k_a, r_k)
    yf, yb = wkv_scan(r, k, v, dec_f, dec_b, icl_f, icl_b, k_k, k_a, n_ctx)
    return rwkv_post(yf, yb, vk, bonus, gate, lnx_g, lnx_b, out_dtype)


def _rope_lanes(z, cos, sin_lo, sin_hi):
    return (z * cos + pltpu.roll(z, LANES - SUBLANES, axis=1) * sin_lo + pltpu.roll(z, SUBLANES, axis=1) * sin_hi)


def _mla_q_kernel(pq_ref, gq_ref, w_ref, m_ref, gain_ref, cos_ref, slo_ref, shi_ref, q_o):
    h = _rms_rows(pq_ref[0], gq_ref[...]).astype(BF16)
    q = jnp.dot(h, w_ref[...], preferred_element_type=F32)
    ms = jnp.dot((q * q).astype(BF16), m_ref[...], preferred_element_type=F32)
    qh = q * lax.rsqrt(ms + EPS) * gain_ref[...]
    cos, slo, shi = cos_ref[...], slo_ref[...], shi_ref[...]
    tiles = [_rope_lanes(qh[:, i * HEAD_PAD:(i + 1) * HEAD_PAD], cos, slo, shi) for i in range(B_HEADS)]
    q_o[0] = (jnp.concatenate(tiles, axis=-1) * ATTN_SCALE).astype(BF16)


def _mla_kv_kernel(pkv_ref, gkv_ref, wk_ref, wv_ref, m_ref, gain_ref, gr_ref, cos_ref, slo_ref, shi_ref,
                   place_ref, k_o, v_o):
    x = pkv_ref[0]
    h = _rms_rows(x[:, :KV_LORA], gkv_ref[...]).astype(BF16)
    kn = jnp.dot(h, wk_ref[...], preferred_element_type=F32)
    v_o[0] = jnp.dot(h, wv_ref[...], preferred_element_type=F32).astype(BF16)
    ms = jnp.dot((kn * kn).astype(BF16), m_ref[...], preferred_element_type=F32)
    knh = kn * lax.rsqrt(ms + EPS) * gain_ref[...]
    kr = x[:, KV_LORA:]
    kr = kr * lax.rsqrt(jnp.sum(kr * kr, axis=-1, keepdims=True) * (1.0 / QK_ROPE) + EPS) * gr_ref[...]
    kr = _rope_lanes(kr, cos_ref[...], slo_ref[...], shi_ref[...]).astype(BF16)
    k_o[0] = (knh + jnp.dot(kr, place_ref[...], preferred_element_type=F32)).astype(BF16)


def _head_pattern(nope, rope):
    tile = jnp.concatenate([nope, rope, jnp.zeros((HEAD_PAD - QK_NOPE - QK_ROPE,), F32)])
    return jnp.tile(tile, B_HEADS)[None, :]


def _seg_mean_matrix(with_rope):
    i = jnp.arange(QP_WIDTH)
    head, off = i // HEAD_PAD, i % HEAD_PAD
    nope = off < QK_NOPE
    rope = jnp.logical_and(off >= QK_NOPE, off < QK_NOPE + QK_ROPE)
    same = head[:, None] == head[None, :]
    m = jnp.where(same & nope[:, None] & nope[None, :], 1.0 / QK_NOPE, 0.0)
    if with_rope:
        m = m + jnp.where(same & rope[:, None] & rope[None, :], 1.0 / QK_ROPE, 0.0)
    return m.astype(BF16)


def _rope_tables(n, n_ctx):
    t = jnp.arange(n, dtype=jnp.int32)
    r_pos = (t // GRID_W).astype(F32)
    c_pos = (t % GRID_W).astype(F32)
    axis_dim = QK_ROPE // 2
    inv_freq = 1.0 / (ROPE_BASE ** (jnp.arange(0, axis_dim, 2, dtype=F32) / axis_dim))
    ang_r = r_pos[:, None] * inv_freq
    ang_c = c_pos[:, None] * inv_freq
    half = axis_dim // 2
    z = jnp.zeros((n, half), F32)
    cos = jnp.concatenate([jnp.cos(ang_r)] * 2 + [jnp.cos(ang_c)] * 2, axis=1)
    s_lo = jnp.concatenate([-jnp.sin(ang_r), z, -jnp.sin(ang_c), z], axis=1)
    s_hi = jnp.concatenate([z, jnp.sin(ang_r), z, jnp.sin(ang_c)], axis=1)

    def widen(tab, fill):
        tab = jnp.concatenate([jnp.full((n_ctx, QK_ROPE), fill, F32), tab], axis=0)
        return jnp.concatenate([jnp.full((n_ctx + n, QK_NOPE), fill, F32), tab,
                                jnp.full((n_ctx + n, HEAD_PAD - QK_NOPE - QK_ROPE), fill, F32)], axis=1)

    return widen(cos, 1.0), widen(s_lo, 0.0), widen(s_hi, 0.0)


def mla_qkv(p_q, p_kv, tabs, q_norm_g, kv_norm_g, w_uq, w_ukv, q_nope_g, k_nope_g, q_rope_g, k_rope_g):
    bsz, t_all, _ = p_q.shape
    nblk = t_all // BM
    dq = QK_NOPE + QK_ROPE
    w_q = jnp.pad(w_uq.reshape(Q_LORA, B_HEADS, dq), ((0, 0), (0, 0), (0, HEAD_PAD - dq))).reshape(
        Q_LORA, QP_WIDTH).astype(BF16)
    w4 = w_ukv.reshape(KV_LORA, B_HEADS, QK_NOPE + V_HEAD)
    w_k = jnp.pad(w4[:, :, :QK_NOPE], ((0, 0), (0, 0), (0, HEAD_PAD - QK_NOPE))).reshape(
        KV_LORA, QP_WIDTH).astype(BF16)
    w_v = w4[:, :, QK_NOPE:].reshape(KV_LORA, B_WIDTH).astype(BF16)
    zero_r = jnp.zeros((QK_ROPE,), F32)
    i = jnp.arange(LANES)
    rope_lane = jnp.logical_and(i >= QK_NOPE, i < QK_NOPE + QK_ROPE)
    place = jnp.where(rope_lane[:, None] & (i[:, None] == (jnp.arange(QP_WIDTH) % HEAD_PAD)[None, :]),
                      1.0, 0.0).astype(BF16)
    gr = jnp.concatenate([jnp.zeros((QK_NOPE,), F32), k_rope_g, jnp.zeros((LANES - QK_NOPE - QK_ROPE,), F32)])

    row_spec = lambda n: pl.BlockSpec((1, BM, n), lambda j, b: (b, j, 0))
    tab_spec = pl.BlockSpec((BM, LANES), lambda j, b: (j, 0))
    c = lambda shape: _const_spec(shape, 2)
    q = pl.pallas_call(
        _mla_q_kernel,
        out_shape=jax.ShapeDtypeStruct((bsz, t_all, QP_WIDTH), BF16),
        grid=(nblk, bsz),
        in_specs=[row_spec(Q_LORA), c((1, Q_LORA)), c((Q_LORA, QP_WIDTH)), c((QP_WIDTH, QP_WIDTH)),
                  c((1, QP_WIDTH)), tab_spec, tab_spec, tab_spec],
        out_specs=row_spec(QP_WIDTH),
        compiler_params=_params(2),
        name="mla_q",
    )(p_q, q_norm_g[None, :], w_q, _seg_mean_matrix(True), _head_pattern(q_nope_g, q_rope_g), *tabs)
    k, v = pl.pallas_call(
        _mla_kv_kernel,
        out_shape=[jax.ShapeDtypeStruct((bsz, t_all, QP_WIDTH), BF16),
                   jax.ShapeDtypeStruct((bsz, t_all, B_WIDTH), BF16)],
        grid=(nblk, bsz),
        in_specs=[row_spec(KV_IN_PAD), c((1, KV_LORA)), c((KV_LORA, QP_WIDTH)), c((KV_LORA, B_WIDTH)),
                  c((QP_WIDTH, QP_WIDTH)), c((1, QP_WIDTH)), c((1, LANES)), tab_spec, tab_spec, tab_spec,
                  c((LANES, QP_WIDTH))],
        out_specs=[row_spec(QP_WIDTH), row_spec(B_WIDTH)],
        compiler_params=_params(2),
        name="mla_kv",
    )(p_kv, kv_norm_g[None, :], w_k, w_v, _seg_mean_matrix(False), _head_pattern(k_nope_g, zero_r),
      gr[None, :], *tabs, place)
    return q, k, v


ATT_BQ = 256
HEADS_PER_STEP = 2


def _attn_kernel(q_ref, k_ref, v_ref, o_ref):
    v = v_ref[0]
    lane = lax.broadcasted_iota(jnp.int32, o_ref.shape[1:], 1)
    out = None
    for h in range(HEADS_PER_STEP):
        q = q_ref[0, :, h * HEAD_PAD:(h + 1) * HEAD_PAD]
        k = k_ref[0, :, h * HEAD_PAD:(h + 1) * HEAD_PAD]
        s = lax.dot_general(q, k, (((1,), (1,)), ((), ())), preferred_element_type=F32)
        p = jnp.exp(s - jnp.max(s, axis=-1, keepdims=True))
        l = jnp.sum(p, axis=-1, keepdims=True)
        o = jnp.dot(p.astype(BF16), v, preferred_element_type=F32) / l
        out = o if out is None else jnp.where(lane < h * V_HEAD, out, o)
    o_ref[0] = out.astype(o_ref.dtype)


def attention(q, k, v, q_row0, nq, nk):
    bsz = q.shape[0]
    bq = min(ATT_BQ, nq)
    qb0 = q_row0 // bq
    pair = HEADS_PER_STEP * HEAD_PAD
    return pl.pallas_call(
        _attn_kernel,
        out_shape=jax.ShapeDtypeStruct((bsz, nq, B_WIDTH), BF16),
        grid=(bsz, B_HEADS // HEADS_PER_STEP, nq // bq),
        in_specs=[pl.BlockSpec((1, bq, pair), lambda b, hp, i: (b, qb0 + i, hp)),
                  pl.BlockSpec((1, nk, pair), lambda b, hp, i: (b, 0, hp)),
                  pl.BlockSpec((1, nk, LANES), lambda b, hp, i: (b, 0, hp))],
        out_specs=pl.BlockSpec((1, bq, LANES), lambda b, hp, i: (b, i, hp)),
        compiler_params=_params(3),
        name="mla_attention",
    )(q, k, v)


def _outproj_kernel(ya_ref, ybc_ref, ybl_ref, pc_ref, pcp_ref, pcn_ref, x_ref, mod_ref, cw_ref, w_ref, o_ref, *,
                    row_off, n_ctx, t_all):
    j = pl.program_id(1) + row_off
    t0 = j * BM
    first = jnp.logical_or(t0 == 0, t0 == n_ctx)
    last = jnp.logical_or(t0 + BM == n_ctx, t0 + BM == t_all)
    o1, o2 = C_WIDTH, 2 * C_WIDTH
    pc = pc_ref[0]
    u = pc[:, o1:o2] * pc[:, o2:]
    pcp = pcp_ref[0, SUBLANES - 1:SUBLANES, :]
    pcn = pcn_ref[0, 0:1, :]
    u_prev = jnp.where(first, 0.0, pcp[:, o1:o2] * pcp[:, o2:])
    u_next = jnp.where(last, 0.0, pcn[:, o1:o2] * pcn[:, o2:])
    rows = lax.broadcasted_iota(jnp.int32, u.shape, 0)
    up = jnp.where(rows == 0, u_prev, pltpu.roll(u, 1, axis=0))
    un = jnp.where(rows == BM - 1, u_next, pltpu.roll(u, BM - 1, axis=0))
    yc = pc[:, :o1] * (cw_ref[0:1, :] * up + cw_ref[1:2, :] * u + cw_ref[2:3, :] * un)
    yb = jnp.where(j == 0, ybc_ref[0], ybl_ref[0]) if row_off == 0 else ybl_ref[0]
    acc = jnp.dot(ya_ref[0], w_ref[0:A_WIDTH, :], preferred_element_type=F32)
    acc += jnp.dot(yb, w_ref[A_WIDTH:A_WIDTH + B_WIDTH, :], preferred_element_type=F32)
    acc += jnp.dot(yc.astype(BF16), w_ref[A_WIDTH + B_WIDTH:, :], preferred_element_type=F32)
    o_ref[0] = x_ref[0] + mod_ref[0, :, 2 * D_MODEL:3 * D_MODEL] * acc


def outproj(ya, yb_ctx, yb_lat, p_c, xa, mod, conv_w, w_out, n_ctx, with_ctx):
    bsz, t_all, d = xa.shape
    row_off = 0 if with_ctx else n_ctx // BM
    nblk = t_all // BM - row_off
    nb8 = t_all // SUBLANES
    per8 = BM // SUBLANES
    rows = lambda n: pl.BlockSpec((1, BM, n), lambda b, j: (b, j + row_off, 0))
    lat_off = n_ctx // BM
    return pl.pallas_call(
        functools.partial(_outproj_kernel, row_off=row_off, n_ctx=n_ctx, t_all=t_all),
        out_shape=jax.ShapeDtypeStruct((bsz, nblk * BM, d), F32),
        grid=(bsz, nblk),
        in_specs=[rows(A_WIDTH),
                  pl.BlockSpec((1, BM, B_WIDTH), lambda b, j: (b, 0, 0)),
                  pl.BlockSpec((1, BM, B_WIDTH), lambda b, j: (b, jnp.maximum(j + row_off - lat_off, 0), 0)),
                  rows(C_IN),
                  pl.BlockSpec((1, SUBLANES, C_IN), lambda b, j: (b, jnp.maximum((j + row_off) * per8 - 1, 0), 0)),
                  pl.BlockSpec((1, SUBLANES, C_IN),
                               lambda b, j: (b, jnp.minimum((j + row_off + 1) * per8, nb8 - 1), 0)),
                  rows(d),
                  pl.BlockSpec((1, 1, 6 * d), lambda b, j: (2 * b + jnp.minimum(j + row_off, 1), 0, 0)),
                  _const_spec((3, C_WIDTH), 2), _const_spec((d, d), 2)],
        out_specs=pl.BlockSpec((1, BM, d), lambda b, j: (b, j, 0)),
        compiler_params=_params(2),
        name="outproj",
    )(ya, yb_ctx, yb_lat, p_c, p_c, p_c, xa, mod, conv_w, w_out)


FFN_CHUNKS = 2


def _ffn_in_kernel(x_ref, mod_ref, g_ref, w_ref, o_ref):
    sh = mod_ref[0, :, 3 * D_MODEL:4 * D_MODEL]
    sc = mod_ref[0, :, 4 * D_MODEL:5 * D_MODEL]
    h = (_rms_rows(x_ref[0], g_ref[...]) * (1.0 + sc) + sh).astype(BF16)
    d_ff = o_ref.shape[-1]
    cw = d_ff // FFN_CHUNKS
    for c in range(FFN_CHUNKS):
        gate = jnp.dot(h, w_ref[:, c * cw:(c + 1) * cw], preferred_element_type=F32)
        up = jnp.dot(h, w_ref[:, d_ff + c * cw:d_ff + (c + 1) * cw], preferred_element_type=F32)
        o_ref[0, :, c * cw:(c + 1) * cw] = (gate * jax.nn.sigmoid(gate) * up).astype(o_ref.dtype)


def _ffn_out_kernel(a_ref, x_ref, mod_ref, w_ref, o_ref):
    acc = jnp.dot(a_ref[0], w_ref[...], preferred_element_type=F32)
    o_ref[0] = x_ref[0] + mod_ref[0, :, 5 * D_MODEL:6 * D_MODEL] * acc


def ffn(x1, mod, g, w_fi, w_fo, lat_only):
    bsz, t, d = x1.shape
    d_ff = w_fo.shape[0]
    seg = 1 if lat_only else 0
    rows = lambda n: pl.BlockSpec((1, BM, n), lambda b, j: (b, j, 0))
    mod_spec = pl.BlockSpec((1, 1, 6 * d), lambda b, j: (2 * b + jnp.minimum(j + seg, 1), 0, 0))
    act = pl.pallas_call(
        _ffn_in_kernel,
        out_shape=jax.ShapeDtypeStruct((bsz, t, d_ff), BF16),
        grid=(bsz, t // BM),
        in_specs=[rows(d), mod_spec, _const_spec((1, d), 2), _const_spec((d, 2 * d_ff), 2)],
        out_specs=rows(d_ff),
        compiler_params=_params(2),
        name="ffn_in",
    )(x1, mod, g[None, :], w_fi)
    return pl.pallas_call(
        _ffn_out_kernel,
        out_shape=jax.ShapeDtypeStruct((bsz, t, d), F32),
        grid=(bsz, t // BM),
        in_specs=[rows(d_ff), rows(d), mod_spec, _const_spec((d_ff, d), 2)],
        out_specs=rows(d),
        compiler_params=_params(2),
        name="ffn_out",
    )(act, x1, mod, w_fo)


def kernel(x, c, ctx, c_ctx, ada_w, ada_b, norm1_g, norm2_g, w_in, tshift_mu, decay_w0, decay_up, icl_a0, icl_up, gate_up, k_k, k_a, r_k, lnx_g, lnx_b, q_norm_g, kv_norm_g, w_uq, w_ukv, q_nope_g, k_nope_g, q_rope_g, k_rope_g, conv_w, w_out, w_ffn_in, w_ffn_out):
    bsz, n, d = x.shape
    n_ctx = ctx.shape[1]
    depth = ada_w.shape[0]
    assert n_ctx == BM and n % BM == 0 and d == D_MODEL
    tabs = _rope_tables(n, n_ctx)

    xa = jnp.concatenate([ctx, x], axis=1)
    cond = jnp.concatenate([c, c_ctx[None, :], jnp.zeros((2 * SUBLANES - bsz - 1, d), c.dtype)], axis=0)
    silu_cond = cond * jax.nn.sigmoid(cond)
    o_kv = A_IN + Q_LORA + KV_LORA
    for l in range(depth):
        last = l == depth - 1
        ada = pmm(silu_cond, ada_w[l]) + ada_b[l]
        mod = jnp.stack([jnp.broadcast_to(ada[bsz], (bsz, 6 * d)), ada[:bsz]], axis=1).reshape(2 * bsz, 1, 6 * d)

        wl = w_in[l]
        w_p = jnp.concatenate(
            [wl[:, :o_kv], jnp.zeros((d, QK_NOPE), F32), wl[:, o_kv:A_IN + B_IN],
             jnp.zeros((d, LANES - QK_NOPE - QK_ROPE), F32), wl[:, A_IN + B_IN:]], axis=1).astype(BF16)
        p_a, p_q, p_kv, p_c = inproj(xa, mod, norm1_g[l], w_p)

        ya = rwkv_group(p_a, n_ctx, tshift_mu[l], decay_w0[l], decay_up[l], icl_a0[l], icl_up[l], gate_up[l],
                        k_k[l], k_a[l], r_k[l], lnx_g[l], lnx_b[l])
        q, k, v = mla_qkv(p_q, p_kv, tabs, q_norm_g[l], kv_norm_g[l], w_uq[l], w_ukv[l], q_nope_g[l],
                          k_nope_g[l], q_rope_g[l], k_rope_g[l])
        yb_lat = attention(q, k, v, n_ctx, n, n_ctx + n)
        yb_ctx = yb_lat if last else attention(q, k, v, 0, n_ctx, n_ctx)

        x1 = outproj(ya, yb_ctx, yb_lat, p_c, xa, mod, conv_w[l], w_out[l].astype(BF16), n_ctx, not last)
        xa = ffn(x1, mod, norm2_g[l], w_ffn_in[l].astype(BF16), w_ffn_out[l].astype(BF16), last)
    return xa
```

```python
import functools

import jax
import jax.numpy as jnp
from jax import lax
from jax.experimental import pallas as pl
from jax.experimental.pallas import tpu as pltpu

D_MODEL = 1024
GRID_W = 64
A_HEADS = 6
A_HEAD_DIM = 64
A_WIDTH = A_HEADS * A_HEAD_DIM
DECAY_LORA = 64
ICL_LORA = 64
GATE_LORA = 128
LOG_DECAY_SCALE = 0.606531
GN_EPS = A_HEAD_DIM * 1e-5
B_HEADS = 6
QK_NOPE = 64
QK_ROPE = 32
V_HEAD = 64
B_WIDTH = B_HEADS * V_HEAD
Q_LORA = 768
KV_LORA = 256
ROPE_BASE = 10000.0
ATTN_SCALE = (QK_NOPE + QK_ROPE) ** -0.5
LOG2_E = 1.4426950408889634
C_WIDTH = D_MODEL - A_WIDTH - B_WIDTH
A_IN = 3 * A_WIDTH + DECAY_LORA + ICL_LORA + GATE_LORA
B_IN = Q_LORA + KV_LORA + QK_ROPE
C_IN = 3 * C_WIDTH
EPS = 1e-6

LANES = 128
SUBLANES = 8
HEAD_PAD = LANES
QP_WIDTH = B_HEADS * HEAD_PAD
KV_IN_PAD = KV_LORA + LANES
P_PAD = A_IN + Q_LORA + KV_IN_PAD + C_IN
VMEM_LIMIT = 48 * 1024 * 1024
BM = 256
F32 = jnp.float32
BF16 = jnp.bfloat16


def _const_spec(shape, nargs):
    zeros = tuple(0 for _ in shape)
    if nargs == 1:
        return pl.BlockSpec(shape, lambda i: zeros)
    if nargs == 2:
        return pl.BlockSpec(shape, lambda i, j: zeros)
    return pl.BlockSpec(shape, lambda i, j, k: zeros)


def _params(n_axes):
    return pltpu.CompilerParams(dimension_semantics=("arbitrary",) * n_axes, vmem_limit_bytes=VMEM_LIMIT)


def _rms_rows(x, g):
    return x * lax.rsqrt(jnp.mean(x * x, axis=-1, keepdims=True) + EPS) * g


def _mm_kernel(x_ref, w_ref, o_ref):
    o_ref[...] = jnp.dot(x_ref[...].astype(BF16), w_ref[...].astype(BF16), preferred_element_type=F32)


def pmm(x, w, bn=1024):
    m, k = x.shape
    n = w.shape[1]
    return pl.pallas_call(
        _mm_kernel,
        out_shape=jax.ShapeDtypeStruct((m, n), F32),
        grid=(n // bn,),
        in_specs=[pl.BlockSpec((m, k), lambda j: (0, 0)), pl.BlockSpec((k, bn), lambda j: (0, j))],
        out_specs=pl.BlockSpec((m, bn), lambda j: (0, j)),
        compiler_params=_params(1),
        name="pmm",
    )(x, w)


def _inproj_kernel(x_ref, mod_ref, g_ref, w_ref, pa_o, pq_o, pkv_o, pc_o):
    sh = mod_ref[0, :, 0:D_MODEL]
    sc = mod_ref[0, :, D_MODEL:2 * D_MODEL]
    h = (_rms_rows(x_ref[0], g_ref[...]) * (1.0 + sc) + sh).astype(BF16)
    o = 0
    for out in (pa_o, pq_o, pkv_o, pc_o):
        n = out.shape[-1]
        out[0] = jnp.dot(h, w_ref[:, o:o + n], preferred_element_type=F32)
        o += n


def inproj(xa, mod, g, w_p):
    bsz, t_all, d = xa.shape
    nblk = t_all // BM
    widths = (A_IN, Q_LORA, KV_IN_PAD, C_IN)
    return pl.pallas_call(
        _inproj_kernel,
        out_shape=[jax.ShapeDtypeStruct((bsz, t_all, n), F32) for n in widths],
        grid=(bsz, nblk),
        in_specs=[pl.BlockSpec((1, BM, d), lambda b, j: (b, j, 0)),
                  pl.BlockSpec((1, 1, 6 * d), lambda b, j: (2 * b + jnp.minimum(j, 1), 0, 0)),
                  _const_spec((1, d), 2), _const_spec((d, P_PAD), 2)],
        out_specs=[pl.BlockSpec((1, BM, n), lambda b, j: (b, j, 0)) for n in widths],
        compiler_params=_params(2),
        name="inproj",
    )(xa, mod, g[None, :], w_p)


def _seg_matrix():
    h = jnp.arange(A_WIDTH) // A_HEAD_DIM
    return (h[:, None] == h[None, :]).astype(BF16)


def _segsum(z, seg):
    hi = z.astype(BF16)
    lo = (z - hi.astype(F32)).astype(BF16)
    return jnp.dot(hi, seg, preferred_element_type=F32) + jnp.dot(lo, seg, preferred_element_type=F32)


RW_TT = 128
CHAIN_PAD = SUBLANES


def _rwkv_prep_kernel(p_ref, pprev_ref, pnext_ref, mu_ref, wlo_ref, wg_ref, bias_ref, ka_ref, rk_ref, seg_ref,
                      r_o, k_o, v_o, df_o, db_o, if_o, ib_o, vk_o, bonus_o, gate_o, pk_s, *, n_ctx, t_all):
    t0 = pl.program_id(1) * RW_TT
    x = p_ref[0]
    first = jnp.logical_or(t0 == 0, t0 == n_ctx)
    last = jnp.logical_or(t0 + RW_TT == n_ctx, t0 + RW_TT == t_all)
    prev_row = jnp.where(first, 0.0, pprev_ref[0, SUBLANES - 1:SUBLANES, :])
    next_row = jnp.where(last, 0.0, pnext_ref[0, 0:1, :])
    rows = lax.broadcasted_iota(jnp.int32, x.shape, 0)
    xp = jnp.where(rows == 0, prev_row, pltpu.roll(x, 1, axis=0))
    xn = jnp.where(rows == RW_TT - 1, next_row, pltpu.roll(x, RW_TT - 1, axis=0))
    xs = x + mu_ref[0:1, :] * (xp - x) + mu_ref[1:2, :] * (xn - x)

    o1, o2, o3 = A_WIDTH, 2 * A_WIDTH, 3 * A_WIDTH
    o5 = o3 + DECAY_LORA + ICL_LORA
    r, k, v = xs[:, :o1], xs[:, o1:o2], xs[:, o2:o3]
    lo2 = xs[:, o3:o5]
    lane = lax.broadcasted_iota(jnp.int32, lo2.shape, 1)
    lo2 = jnp.where(lane < DECAY_LORA, jnp.tanh(lo2), lo2).astype(BF16)
    lo_out = jnp.dot(lo2, wlo_ref[...], preferred_element_type=F32) + bias_ref[...]
    gate = jnp.dot(jax.nn.sigmoid(xs[:, o5:]).astype(BF16), wg_ref[...], preferred_element_type=F32)
    dec_f, dec_b = lo_out[:, :o1], lo_out[:, o1:o2]
    icl_f, icl_b = lo_out[:, o2:o3], lo_out[:, o3:]
    m = 2.0 + (jax.nn.sigmoid(icl_f) + jax.nn.sigmoid(icl_b) - 2.0) * ka_ref[...]
    rkm = r * k * m
    seg = seg_ref[...]
    vk_o[0] = v * _segsum(rkm, seg)
    bonus_o[0] = v * _segsum(rkm * rk_ref[...], seg)
    gate_o[0] = gate
    half = RW_TT // 2
    zero = jnp.zeros((half, LANES), F32)
    for i, (val, out) in enumerate(((r, r_o), (k, k_o), (v, v_o), (dec_f, df_o), (dec_b, db_o),
                                    (icl_f, if_o), (icl_b, ib_o))):
        heads = []
        for cb in range(A_WIDTH // LANES):
            pk_s[i, cb] = val[:, cb * LANES:(cb + 1) * LANES]
            even = pk_s[i, cb, pl.ds(0, half, stride=2), :]
            odd = pk_s[i, cb, pl.ds(1, half, stride=2), :]
            for hh in range(2):
                sl = slice(hh * A_HEAD_DIM, (hh + 1) * A_HEAD_DIM)
                heads.append(jnp.concatenate([even[:, sl], odd[:, sl]], axis=-1))
        heads += [zero] * (CHAIN_PAD - A_HEADS)
        out[...] = jnp.swapaxes(jnp.stack(heads, axis=0), 0, 1)


def rwkv_prep(p_a, n_ctx, tshift_mu, decay_w0, decay_up, icl_a0, icl_up, gate_up, k_a, r_k):
    bsz, t_all, _ = p_a.shape
    assert n_ctx % RW_TT == 0 and t_all % RW_TT == 0
    zeros = jnp.zeros((DECAY_LORA, 2 * A_WIDTH), F32)
    wlo = jnp.concatenate([
        jnp.concatenate([decay_up[0], decay_up[1], zeros], axis=1),
        jnp.concatenate([zeros, icl_up[0], icl_up[1]], axis=1)], axis=0).astype(BF16)
    bias = jnp.concatenate([decay_w0[0], decay_w0[1], icl_a0[0], icl_a0[1]])[None, :]
    nb8 = t_all // SUBLANES
    per8 = RW_TT // SUBLANES
    cm = jax.ShapeDtypeStruct((t_all // 2, bsz * CHAIN_PAD, LANES), F32)
    tm = jax.ShapeDtypeStruct((bsz, t_all, A_WIDTH), F32)
    cm_spec = pl.BlockSpec((RW_TT // 2, CHAIN_PAD, LANES), lambda b, j: (j, b, 0))
    tm_spec = pl.BlockSpec((1, RW_TT, A_WIDTH), lambda b, j: (b, j, 0))
    return pl.pallas_call(
        functools.partial(_rwkv_prep_kernel, n_ctx=n_ctx, t_all=t_all),
        out_shape=[cm] * 7 + [tm] * 3,
        grid=(bsz, t_all // RW_TT),
        in_specs=[pl.BlockSpec((1, RW_TT, A_IN), lambda b, j: (b, j, 0)),
                  pl.BlockSpec((1, SUBLANES, A_IN), lambda b, j: (b, jnp.maximum(j * per8 - 1, 0), 0)),
                  pl.BlockSpec((1, SUBLANES, A_IN), lambda b, j: (b, jnp.minimum((j + 1) * per8, nb8 - 1), 0)),
                  _const_spec((2, A_IN), 2), _const_spec((2 * DECAY_LORA, 4 * A_WIDTH), 2),
                  _const_spec((GATE_LORA, A_WIDTH), 2), _const_spec((1, 4 * A_WIDTH), 2),
                  _const_spec((1, A_WIDTH), 2), _const_spec((1, A_WIDTH), 2), _const_spec((A_WIDTH, A_WIDTH), 2)],
        out_specs=[cm_spec] * 7 + [tm_spec] * 3,
        scratch_shapes=[pltpu.VMEM((7, A_WIDTH // LANES, RW_TT, LANES), F32)],
        compiler_params=_params(2),
        name="rwkv_prep",
    )(p_a, p_a, p_a, tshift_mu, wlo, gate_up.astype(BF16), bias, k_a[None, :], r_k.reshape(1, A_WIDTH),
      _seg_matrix())


WKV_TB = 16
WKV_ROWS = WKV_TB // 2
WKV_ACC = 2
WKV_VSPLIT = 2


def _wkv_kernel(rf_ref, rb_ref, kf_ref, kb_ref, vf_ref, vb_ref, df_ref, db_ref, if_ref, ib_ref,
                kk_ref, ka_ref, yf_ref, yb_ref,
                s_ref, a_s, b_s, k_s, rp_s, v_s, y_s, w_s, gp_s, gi_s, g_s, nat_s, *, n_chain):
    @pl.when(pl.program_id(0) == 0)
    def _():
        s_ref[...] = jnp.zeros_like(s_ref)

    lane = lax.broadcasted_iota(jnp.int32, (A_HEAD_DIM, LANES), 1)
    is_fwd = lane < n_chain

    srcs = ((rf_ref, rb_ref), (kf_ref, kb_ref), (vf_ref, vb_ref), (if_ref, ib_ref), (df_ref, db_ref))
    q_dec = len(srcs) - 1

    def to_tiles(q, j):
        f_ref, b_ref = srcs[q]
        x = jnp.concatenate([f_ref[j], b_ref[j]], axis=0)
        nat_s[q, 2 * j:2 * j + 2] = x.T.reshape(2, A_HEAD_DIM, LANES)

    def step_tile(q, i):
        return jnp.where(is_fwd, nat_s[q, i], nat_s[q, WKV_TB - 1 - i])

    for j in range(WKV_ROWS):
        to_tiles(q_dec, j)
    g = None
    for i in range(WKV_TB):
        w = jnp.exp(-LOG_DECAY_SCALE * jax.nn.sigmoid(step_tile(q_dec, i)))
        g = w if g is None else g * w
        w_s[i] = w
        gi_s[i] = 1.0 / g
        if i + 1 < WKV_TB:
            gp_s[i + 1] = g
    g_s[...] = g

    kk_gain = kk_ref[...]
    ka = ka_ref[...]
    for ja in range(WKV_ROWS // 2):
        jb = WKV_ROWS - 1 - ja
        for q in range(q_dec):
            to_tiles(q, ja)
            to_tiles(q, jb)
        for i in (2 * ja, 2 * ja + 1, 2 * jb, 2 * jb + 1):
            r, k, v, icl = (step_tile(q, i) for q in range(q_dec))
            icl = jax.nn.sigmoid(icl)
            g_inv = gi_s[i]
            kk = k * kk_gain
            kk = kk * lax.rsqrt(jnp.sum(kk * kk, axis=0, keepdims=True) + 1e-12)
            b = kk * icl
            rp = w_s[i] * r - kk * jnp.sum(b * r, axis=0, keepdims=True)
            v_s[i] = v
            a_s[i] = -kk if i == 0 else -kk * gp_s[i]
            rp_s[i] = rp if i == 0 else rp * gp_s[i]
            b_s[i] = b * g_inv
            k_s[i] = k * (1.0 + (icl - 1.0) * ka) * g_inv

    def step(t, carry):
        rows = A_HEAD_DIM // WKV_VSPLIT
        for part in range(WKV_VSPLIT):
            vs = slice(part * rows, (part + 1) * rows)
            sa_p = [None] * WKV_ACC
            y_p = [None] * WKV_ACC
            for kx in range(A_HEAD_DIM):
                s = s_ref[kx, vs, :]
                pa = s * a_s[t, kx:kx + 1, :]
                py = s * rp_s[t, kx:kx + 1, :]
                i = kx % WKV_ACC
                sa_p[i] = pa if sa_p[i] is None else sa_p[i] + pa
                y_p[i] = py if y_p[i] is None else y_p[i] + py
            sa = functools.reduce(lambda u, w: u + w, sa_p)
            y_s[t, vs, :] = functools.reduce(lambda u, w: u + w, y_p)
            vt = v_s[t, vs, :]
            for kx in range(A_HEAD_DIM):
                s_ref[kx, vs, :] = s_ref[kx, vs, :] + sa * b_s[t, kx:kx + 1, :] + vt * k_s[t, kx:kx + 1, :]
        return carry

    lax.fori_loop(0, WKV_TB, step, 0)

    for kx in range(A_HEAD_DIM):
        s_ref[kx] = s_ref[kx] * g_s[kx:kx + 1, :]

    for j in range(WKV_ROWS):
        y = jnp.concatenate([jnp.where(is_fwd, y_s[i], y_s[WKV_TB - 1 - i]) for i in (2 * j, 2 * j + 1)], axis=0)
        yt = y.T
        yf_ref[j] = yt[:n_chain]
        yb_ref[j] = yt[n_chain:]


def wkv_scan(r, k, v, dec_f, dec_b, icl_f, icl_b, k_k, k_a, n_ctx):
    t_half, n_chain, _ = r.shape
    t_all, kd = 2 * t_half, A_HEAD_DIM
    assert n_ctx % WKV_TB == 0 and t_all % WKV_TB == 0 and 2 * n_chain == LANES
    nb_ctx = n_ctx // WKV_TB
    nb = t_all // WKV_TB

    def bwd_block(j):
        return jnp.where(j < nb_ctx, nb_ctx - 1 - j, nb + nb_ctx - 1 - j)

    f_spec = pl.BlockSpec((WKV_ROWS, n_chain, LANES), lambda j: (j, 0, 0))
    b_spec = pl.BlockSpec((WKV_ROWS, n_chain, LANES), lambda j: (bwd_block(j), 0, 0))
    tile_spec = pl.BlockSpec((kd, LANES), lambda j: (0, 0))

    def lane_tile(p):
        t = jnp.pad(p.reshape(A_HEADS, kd).T, ((0, 0), (0, CHAIN_PAD - A_HEADS)))
        return jnp.tile(t, (1, LANES // CHAIN_PAD))

    seq = pltpu.VMEM((WKV_TB, kd, LANES), F32)
    out = jax.ShapeDtypeStruct((t_half, n_chain, LANES), F32)
    yf, yb = pl.pallas_call(
        functools.partial(_wkv_kernel, n_chain=n_chain),
        out_shape=[out, out],
        grid=(nb,),
        in_specs=[f_spec, b_spec] * 5 + [tile_spec, tile_spec],
        out_specs=[f_spec, b_spec],
        scratch_shapes=[pltpu.VMEM((kd, kd, LANES), F32)] + [seq] * 9 + [
            pltpu.VMEM((kd, LANES), F32), pltpu.VMEM((5, WKV_TB, kd, LANES), F32)],
        compiler_params=_params(1),
        name="wkv_scan",
    )(r, r, k, k, v, v, dec_f, dec_b, icl_f, icl_b, lane_tile(k_k), lane_tile(k_a))
    return yf, yb


def _rwkv_post_kernel(yf_ref, yb_ref, vk_ref, bonus_ref, gate_ref, g_ref, b_ref, seg_ref, o_ref, y_s):
    half = RW_TT // 2
    yh = jnp.swapaxes(yf_ref[...] + yb_ref[...], 0, 1)
    for cb in range(A_WIDTH // LANES):
        y0 = yh[2 * cb]
        y1 = yh[2 * cb + 1]
        y_s[cb, pl.ds(0, half, stride=2), :] = jnp.concatenate([y0[:, :A_HEAD_DIM], y1[:, :A_HEAD_DIM]], axis=-1)
        y_s[cb, pl.ds(1, half, stride=2), :] = jnp.concatenate([y0[:, A_HEAD_DIM:], y1[:, A_HEAD_DIM:]], axis=-1)
    y = jnp.concatenate([y_s[cb] for cb in range(A_WIDTH // LANES)], axis=-1) + vk_ref[0]
    seg = seg_ref[...]
    mu = _segsum(y, seg) * (1.0 / A_HEAD_DIM)
    d = y - mu
    var = _segsum(d * d, seg) * (1.0 / A_HEAD_DIM)
    yn = d * lax.rsqrt(var + GN_EPS) * g_ref[...] + b_ref[...]
    o_ref[0] = ((yn + bonus_ref[0]) * gate_ref[0]).astype(o_ref.dtype)


def rwkv_post(yf, yb, vk, bonus, gate, lnx_g, lnx_b, out_dtype=BF16):
    bsz, t_all, _ = vk.shape
    cm_spec = pl.BlockSpec((RW_TT // 2, CHAIN_PAD, LANES), lambda b, j: (j, b, 0))
    tm_spec = pl.BlockSpec((1, RW_TT, A_WIDTH), lambda b, j: (b, j, 0))
    return pl.pallas_call(
        _rwkv_post_kernel,
        out_shape=jax.ShapeDtypeStruct((bsz, t_all, A_WIDTH), out_dtype),
        grid=(bsz, t_all // RW_TT),
        in_specs=[cm_spec, cm_spec, tm_spec, tm_spec, tm_spec, _const_spec((1, A_WIDTH), 2),
                  _const_spec((1, A_WIDTH), 2), _const_spec((A_WIDTH, A_WIDTH), 2)],
        out_specs=tm_spec,
        scratch_shapes=[pltpu.VMEM((A_WIDTH // LANES, RW_TT, LANES), F32)],
        compiler_params=_params(2),
        name="rwkv_post",
    )(yf, yb, vk, bonus, gate, lnx_g[None, :], lnx_b[None, :], _seg_matrix())


def rwkv_group(p_a, n_ctx, tshift_mu, decay_w0, decay_up, icl_a0, icl_up, gate_up, k_k, k_a, r_k, lnx_g, lnx_b,
               out_dtype=BF16):
    r, k, v, dec_f, dec_b, icl_f, icl_b, vk, bonus, gate = rwkv_prep(
        p_a, n_ctx, tshift_mu, decay_w0, decay_up, icl_a0, icl_up, gate_up, k_a, r_k)
    yf, yb = wkv_scan(r, k, v, dec_f, dec_b, icl_f, icl_b, k_k, k_a, n_ctx)
    return rwkv_post(yf, yb, vk, bonus, gate, lnx_g, lnx_b, out_dtype)


def _rope_lanes(z, cos, sin_lo, sin_hi):
    return (z * cos + pltpu.roll(z, LANES - SUBLANES, axis=1) * sin_lo + pltpu.roll(z, SUBLANES, axis=1) * sin_hi)


def _mla_q_kernel(pq_ref, gq_ref, w_ref, m_ref, gain_ref, cos_ref, slo_ref, shi_ref, q_o):
    h = _rms_rows(pq_ref[0], gq_ref[...]).astype(BF16)
    q = jnp.dot(h, w_ref[...], preferred_element_type=F32)
    ms = jnp.dot((q * q).astype(BF16), m_ref[...], preferred_element_type=F32)
    qh = q * lax.rsqrt(ms + EPS) * gain_ref[...]
    cos, slo, shi = cos_ref[...], slo_ref[...], shi_ref[...]
    tiles = [_rope_lanes(qh[:, i * HEAD_PAD:(i + 1) * HEAD_PAD], cos, slo, shi) for i in range(B_HEADS)]
    q_o[0] = (jnp.concatenate(tiles, axis=-1) * (ATTN_SCALE * LOG2_E)).astype(BF16)


def _mla_kv_kernel(pkv_ref, gkv_ref, wk_ref, wv_ref, m_ref, gain_ref, gr_ref, cos_ref, slo_ref, shi_ref,
                   place_ref, k_o, v_o):
    x = pkv_ref[0]
    h = _rms_rows(x[:, :KV_LORA], gkv_ref[...]).astype(BF16)
    kn = jnp.dot(h, wk_ref[...], preferred_element_type=F32)
    v = jnp.dot(h, wv_ref[...], preferred_element_type=F32)
    lane = lax.broadcasted_iota(jnp.int32, (v.shape[0], LANES), 1)
    tiles = []
    for pair in range(B_HEADS // 2):
        vp = v[:, pair * LANES:(pair + 1) * LANES]
        tiles += [jnp.where(lane < V_HEAD, vp, 1.0), jnp.where(lane < V_HEAD, 1.0, vp)]
    v_o[0] = jnp.concatenate(tiles, axis=-1).astype(BF16)
    ms = jnp.dot((kn * kn).astype(BF16), m_ref[...], preferred_element_type=F32)
    knh = kn * lax.rsqrt(ms + EPS) * gain_ref[...]
    kr = x[:, KV_LORA:]
    kr = kr * lax.rsqrt(jnp.sum(kr * kr, axis=-1, keepdims=True) * (1.0 / QK_ROPE) + EPS) * gr_ref[...]
    kr = _rope_lanes(kr, cos_ref[...], slo_ref[...], shi_ref[...]).astype(BF16)
    k_o[0] = (knh + jnp.dot(kr, place_ref[...], preferred_element_type=F32)).astype(BF16)


def _head_pattern(nope, rope):
    tile = jnp.concatenate([nope, rope, jnp.zeros((HEAD_PAD - QK_NOPE - QK_ROPE,), F32)])
    return jnp.tile(tile, B_HEADS)[None, :]


def _seg_mean_matrix(with_rope):
    i = jnp.arange(QP_WIDTH)
    head, off = i // HEAD_PAD, i % HEAD_PAD
    nope = off < QK_NOPE
    rope = jnp.logical_and(off >= QK_NOPE, off < QK_NOPE + QK_ROPE)
    same = head[:, None] == head[None, :]
    m = jnp.where(same & nope[:, None] & nope[None, :], 1.0 / QK_NOPE, 0.0)
    if with_rope:
        m = m + jnp.where(same & rope[:, None] & rope[None, :], 1.0 / QK_ROPE, 0.0)
    return m.astype(BF16)


def _rope_tables(n, n_ctx):
    t = jnp.arange(n, dtype=jnp.int32)
    r_pos = (t // GRID_W).astype(F32)
    c_pos = (t % GRID_W).astype(F32)
    axis_dim = QK_ROPE // 2
    inv_freq = 1.0 / (ROPE_BASE ** (jnp.arange(0, axis_dim, 2, dtype=F32) / axis_dim))
    ang_r = r_pos[:, None] * inv_freq
    ang_c = c_pos[:, None] * inv_freq
    half = axis_dim // 2
    z = jnp.zeros((n, half), F32)
    cos = jnp.concatenate([jnp.cos(ang_r)] * 2 + [jnp.cos(ang_c)] * 2, axis=1)
    s_lo = jnp.concatenate([-jnp.sin(ang_r), z, -jnp.sin(ang_c), z], axis=1)
    s_hi = jnp.concatenate([z, jnp.sin(ang_r), z, jnp.sin(ang_c)], axis=1)

    def widen(tab, fill):
        tab = jnp.concatenate([jnp.full((n_ctx, QK_ROPE), fill, F32), tab], axis=0)
        return jnp.concatenate([jnp.full((n_ctx + n, QK_NOPE), fill, F32), tab,
                                jnp.full((n_ctx + n, HEAD_PAD - QK_NOPE - QK_ROPE), fill, F32)], axis=1)

    return widen(cos, 1.0), widen(s_lo, 0.0), widen(s_hi, 0.0)


def mla_qkv(p_q, p_kv, tabs, q_norm_g, kv_norm_g, w_uq, w_ukv, q_nope_g, k_nope_g, q_rope_g, k_rope_g):
    bsz, t_all, _ = p_q.shape
    nblk = t_all // BM
    dq = QK_NOPE + QK_ROPE
    w_q = jnp.pad(w_uq.reshape(Q_LORA, B_HEADS, dq), ((0, 0), (0, 0), (0, HEAD_PAD - dq))).reshape(
        Q_LORA, QP_WIDTH).astype(BF16)
    w4 = w_ukv.reshape(KV_LORA, B_HEADS, QK_NOPE + V_HEAD)
    w_k = jnp.pad(w4[:, :, :QK_NOPE], ((0, 0), (0, 0), (0, HEAD_PAD - QK_NOPE))).reshape(
        KV_LORA, QP_WIDTH).astype(BF16)
    w_v = w4[:, :, QK_NOPE:].reshape(KV_LORA, B_WIDTH).astype(BF16)
    zero_r = jnp.zeros((QK_ROPE,), F32)
    i = jnp.arange(LANES)
    rope_lane = jnp.logical_and(i >= QK_NOPE, i < QK_NOPE + QK_ROPE)
    place = jnp.where(rope_lane[:, None] & (i[:, None] == (jnp.arange(QP_WIDTH) % HEAD_PAD)[None, :]),
                      1.0, 0.0).astype(BF16)
    gr = jnp.concatenate([jnp.zeros((QK_NOPE,), F32), k_rope_g, jnp.zeros((LANES - QK_NOPE - QK_ROPE,), F32)])

    row_spec = lambda n: pl.BlockSpec((1, BM, n), lambda j, b: (b, j, 0))
    tab_spec = pl.BlockSpec((BM, LANES), lambda j, b: (j, 0))
    c = lambda shape: _const_spec(shape, 2)
    q = pl.pallas_call(
        _mla_q_kernel,
        out_shape=jax.ShapeDtypeStruct((bsz, t_all, QP_WIDTH), BF16),
        grid=(nblk, bsz),
        in_specs=[row_spec(Q_LORA), c((1, Q_LORA)), c((Q_LORA, QP_WIDTH)), c((QP_WIDTH, QP_WIDTH)),
                  c((1, QP_WIDTH)), tab_spec, tab_spec, tab_spec],
        out_specs=row_spec(QP_WIDTH),
        compiler_params=_params(2),
        name="mla_q",
    )(p_q, q_norm_g[None, :], w_q, _seg_mean_matrix(True), _head_pattern(q_nope_g, q_rope_g), *tabs)
    k, v = pl.pallas_call(
        _mla_kv_kernel,
        out_shape=[jax.ShapeDtypeStruct((bsz, t_all, QP_WIDTH), BF16),
                   jax.ShapeDtypeStruct((bsz, t_all, QP_WIDTH), BF16)],
        grid=(nblk, bsz),
        in_specs=[row_spec(KV_IN_PAD), c((1, KV_LORA)), c((KV_LORA, QP_WIDTH)), c((KV_LORA, B_WIDTH)),
                  c((QP_WIDTH, QP_WIDTH)), c((1, QP_WIDTH)), c((1, LANES)), tab_spec, tab_spec, tab_spec,
                  c((LANES, QP_WIDTH))],
        out_specs=[row_spec(QP_WIDTH), row_spec(QP_WIDTH)],
        compiler_params=_params(2),
        name="mla_kv",
    )(p_kv, kv_norm_g[None, :], w_k, w_v, _seg_mean_matrix(False), _head_pattern(k_nope_g, zero_r),
      gr[None, :], *tabs, place)
    return q, k, v


ATT_BQ = 256
HEADS_PER_STEP = 2


def _attn_kernel(q_ref, k_ref, v_ref, o_ref):
    scores = []
    for h in range(HEADS_PER_STEP):
        q = q_ref[0, :, h * HEAD_PAD:(h + 1) * HEAD_PAD]
        k = k_ref[0, :, h * HEAD_PAD:(h + 1) * HEAD_PAD]
        scores.append(lax.dot_general(q, k, (((1,), (1,)), ((), ())), preferred_element_type=F32))
    outs = []
    for h in range(HEADS_PER_STEP):
        s = scores[h]
        p = jnp.exp2(s - jnp.max(s, axis=-1, keepdims=True)).astype(BF16)
        outs.append(jnp.dot(p, v_ref[0, :, h * HEAD_PAD:(h + 1) * HEAD_PAD], preferred_element_type=F32))
    lane = lax.broadcasted_iota(jnp.int32, outs[0].shape, 1)
    first = lane < V_HEAD
    num = jnp.where(first, outs[0], outs[1])
    den = pltpu.roll(jnp.where(first, outs[1], outs[0]), V_HEAD, axis=1)
    o_ref[0] = (num / den).astype(o_ref.dtype)


def attention(q, k, v, q_row0, nq, nk):
    bsz = q.shape[0]
    bq = min(ATT_BQ, nq)
    qb0 = q_row0 // bq
    pair = HEADS_PER_STEP * HEAD_PAD
    return pl.pallas_call(
        _attn_kernel,
        out_shape=jax.ShapeDtypeStruct((bsz, nq, B_WIDTH), BF16),
        grid=(bsz, B_HEADS // HEADS_PER_STEP, nq // bq),
        in_specs=[pl.BlockSpec((1, bq, pair), lambda b, hp, i: (b, qb0 + i, hp)),
                  pl.BlockSpec((1, nk, pair), lambda b, hp, i: (b, 0, hp)),
                  pl.BlockSpec((1, nk, pair), lambda b, hp, i: (b, 0, hp))],
        out_specs=pl.BlockSpec((1, bq, LANES), lambda b, hp, i: (b, i, hp)),
        compiler_params=_params(3),
        name="mla_attention",
    )(q, k, v)


def _outproj_kernel(ya_ref, ybc_ref, ybl_ref, pc_ref, pcp_ref, pcn_ref, x_ref, mod_ref, cw_ref, w_ref, o_ref, *,
                    row_off, n_ctx, t_all):
    j = pl.program_id(1) + row_off
    t0 = j * BM
    first = jnp.logical_or(t0 == 0, t0 == n_ctx)
    last = jnp.logical_or(t0 + BM == n_ctx, t0 + BM == t_all)
    o1, o2 = C_WIDTH, 2 * C_WIDTH
    pc = pc_ref[0]
    u = pc[:, o1:o2] * pc[:, o2:]
    pcp = pcp_ref[0, SUBLANES - 1:SUBLANES, :]
    pcn = pcn_ref[0, 0:1, :]
    u_prev = jnp.where(first, 0.0, pcp[:, o1:o2] * pcp[:, o2:])
    u_next = jnp.where(last, 0.0, pcn[:, o1:o2] * pcn[:, o2:])
    rows = lax.broadcasted_iota(jnp.int32, u.shape, 0)
    up = jnp.where(rows == 0, u_prev, pltpu.roll(u, 1, axis=0))
    un = jnp.where(rows == BM - 1, u_next, pltpu.roll(u, BM - 1, axis=0))
    yc = pc[:, :o1] * (cw_ref[0:1, :] * up + cw_ref[1:2, :] * u + cw_ref[2:3, :] * un)
    yb = jnp.where(j == 0, ybc_ref[0], ybl_ref[0]) if row_off == 0 else ybl_ref[0]
    acc = jnp.dot(ya_ref[0], w_ref[0:A_WIDTH, :], preferred_element_type=F32)
    acc += jnp.dot(yb, w_ref[A_WIDTH:A_WIDTH + B_WIDTH, :], preferred_element_type=F32)
    acc += jnp.dot(yc.astype(BF16), w_ref[A_WIDTH + B_WIDTH:, :], preferred_element_type=F32)
    o_ref[0] = x_ref[0] + mod_ref[0, :, 2 * D_MODEL:3 * D_MODEL] * acc


def outproj(ya, yb_ctx, yb_lat, p_c, xa, mod, conv_w, w_out, n_ctx, with_ctx):
    bsz, t_all, d = xa.shape
    row_off = 0 if with_ctx else n_ctx // BM
    nblk = t_all // BM - row_off
    nb8 = t_all // SUBLANES
    per8 = BM // SUBLANES
    rows = lambda n: pl.BlockSpec((1, BM, n), lambda b, j: (b, j + row_off, 0))
    lat_off = n_ctx // BM
    return pl.pallas_call(
        functools.partial(_outproj_kernel, row_off=row_off, n_ctx=n_ctx, t_all=t_all),
        out_shape=jax.ShapeDtypeStruct((bsz, nblk * BM, d), F32),
        grid=(bsz, nblk),
        in_specs=[rows(A_WIDTH),
                  pl.BlockSpec((1, BM, B_WIDTH), lambda b, j: (b, 0, 0)),
                  pl.BlockSpec((1, BM, B_WIDTH), lambda b, j: (b, jnp.maximum(j + row_off - lat_off, 0), 0)),
                  rows(C_IN),
                  pl.BlockSpec((1, SUBLANES, C_IN), lambda b, j: (b, jnp.maximum((j + row_off) * per8 - 1, 0), 0)),
                  pl.BlockSpec((1, SUBLANES, C_IN),
                               lambda b, j: (b, jnp.minimum((j + row_off + 1) * per8, nb8 - 1), 0)),
                  rows(d),
                  pl.BlockSpec((1, 1, 6 * d), lambda b, j: (2 * b + jnp.minimum(j + row_off, 1), 0, 0)),
                  _const_spec((3, C_WIDTH), 2), _const_spec((d, d), 2)],
        out_specs=pl.BlockSpec((1, BM, d), lambda b, j: (b, j, 0)),
        compiler_params=_params(2),
        name="outproj",
    )(ya, yb_ctx, yb_lat, p_c, p_c, p_c, xa, mod, conv_w, w_out)


FFN_CHUNKS = 2


def _ffn_in_kernel(x_ref, mod_ref, g_ref, w_ref, o_ref):
    sh = mod_ref[0, :, 3 * D_MODEL:4 * D_MODEL]
    sc = mod_ref[0, :, 4 * D_MODEL:5 * D_MODEL]
    h = (_rms_rows(x_ref[0], g_ref[...]) * (1.0 + sc) + sh).astype(BF16)
    d_ff = o_ref.shape[-1]
    cw = d_ff // FFN_CHUNKS
    for c in range(FFN_CHUNKS):
        gate = jnp.dot(h, w_ref[:, c * cw:(c + 1) * cw], preferred_element_type=F32)
        up = jnp.dot(h, w_ref[:, d_ff + c * cw:d_ff + (c + 1) * cw], preferred_element_type=F32)
        o_ref[0, :, c * cw:(c + 1) * cw] = (gate * jax.nn.sigmoid(gate) * up).astype(o_ref.dtype)


def _ffn_out_kernel(a_ref, x_ref, mod_ref, w_ref, o_ref):
    acc = jnp.dot(a_ref[0], w_ref[...], preferred_element_type=F32)
    o_ref[0] = x_ref[0] + mod_ref[0, :, 5 * D_MODEL:6 * D_MODEL] * acc


def ffn(x1, mod, g, w_fi, w_fo, lat_only):
    bsz, t, d = x1.shape
    d_ff = w_fo.shape[0]
    seg = 1 if lat_only else 0
    rows = lambda n: pl.BlockSpec((1, BM, n), lambda b, j: (b, j, 0))
    mod_spec = pl.BlockSpec((1, 1, 6 * d), lambda b, j: (2 * b + jnp.minimum(j + seg, 1), 0, 0))
    act = pl.pallas_call(
        _ffn_in_kernel,
        out_shape=jax.ShapeDtypeStruct((bsz, t, d_ff), BF16),
        grid=(bsz, t // BM),
        in_specs=[rows(d), mod_spec, _const_spec((1, d), 2), _const_spec((d, 2 * d_ff), 2)],
        out_specs=rows(d_ff),
        compiler_params=_params(2),
        name="ffn_in",
    )(x1, mod, g[None, :], w_fi)
    return pl.pallas_call(
        _ffn_out_kernel,
        out_shape=jax.ShapeDtypeStruct((bsz, t, d), F32),
        grid=(bsz, t // BM),
        in_specs=[rows(d_ff), rows(d), mod_spec, _const_spec((d_ff, d), 2)],
        out_specs=rows(d),
        compiler_params=_params(2),
        name="ffn_out",
    )(act, x1, mod, w_fo)


def kernel(x, c, ctx, c_ctx, ada_w, ada_b, norm1_g, norm2_g, w_in, tshift_mu, decay_w0, decay_up, icl_a0, icl_up, gate_up, k_k, k_a, r_k, lnx_g, lnx_b, q_norm_g, kv_norm_g, w_uq, w_ukv, q_nope_g, k_nope_g, q_rope_g, k_rope_g, conv_w, w_out, w_ffn_in, w_ffn_out):
    bsz, n, d = x.shape
    n_ctx = ctx.shape[1]
    depth = ada_w.shape[0]
    assert n_ctx == BM and n % BM == 0 and d == D_MODEL
    tabs = _rope_tables(n, n_ctx)

    xa = jnp.concatenate([ctx, x], axis=1)
    cond = jnp.concatenate([c, c_ctx[None, :], jnp.zeros((2 * SUBLANES - bsz - 1, d), c.dtype)], axis=0)
    silu_cond = cond * jax.nn.sigmoid(cond)
    o_kv = A_IN + Q_LORA + KV_LORA
    for l in range(depth):
        last = l == depth - 1
        ada = pmm(silu_cond, ada_w[l]) + ada_b[l]
        mod = jnp.stack([jnp.broadcast_to(ada[bsz], (bsz, 6 * d)), ada[:bsz]], axis=1).reshape(2 * bsz, 1, 6 * d)

        wl = w_in[l]
        w_p = jnp.concatenate(
            [wl[:, :o_kv], jnp.zeros((d, QK_NOPE), F32), wl[:, o_kv:A_IN + B_IN],
             jnp.zeros((d, LANES - QK_NOPE - QK_ROPE), F32), wl[:, A_IN + B_IN:]], axis=1).astype(BF16)
        p_a, p_q, p_kv, p_c = inproj(xa, mod, norm1_g[l], w_p)

        ya = rwkv_group(p_a, n_ctx, tshift_mu[l], decay_w0[l], decay_up[l], icl_a0[l], icl_up[l], gate_up[l],
                        k_k[l], k_a[l], r_k[l], lnx_g[l], lnx_b[l])
        q, k, v = mla_qkv(p_q, p_kv, tabs, q_norm_g[l], kv_norm_g[l], w_uq[l], w_ukv[l], q_nope_g[l],
                          k_nope_g[l], q_rope_g[l], k_rope_g[l])
        yb_lat = attention(q, k, v, n_ctx, n, n_ctx + n)
        yb_ctx = yb_lat if last else attention(q, k, v, 0, n_ctx, n_ctx)

        x1 = outproj(ya, yb_ctx, yb_lat, p_c, xa, mod, conv_w[l], w_out[l].astype(BF16), n_ctx, not last)
        xa = ffn(x1, mod, norm2_g[l], w_ffn_in[l].astype(BF16), w_ffn_out[l].astype(BF16), last)
    return xa
```

```python
import functools

import jax
import jax.numpy as jnp
from jax import lax
from jax.experimental import pallas as pl
from jax.experimental.pallas import tpu as pltpu

D_MODEL = 1024
GRID_W = 64
A_HEADS = 6
A_HEAD_DIM = 64
A_WIDTH = A_HEADS * A_HEAD_DIM
DECAY_LORA = 64
ICL_LORA = 64
GATE_LORA = 128
LOG_DECAY_SCALE = 0.606531
GN_EPS = A_HEAD_DIM * 1e-5
B_HEADS = 6
QK_NOPE = 64
QK_ROPE = 32
V_HEAD = 64
B_WIDTH = B_HEADS * V_HEAD
Q_LORA = 768
KV_LORA = 256
ROPE_BASE = 10000.0
ATTN_SCALE = (QK_NOPE + QK_ROPE) ** -0.5
LOG2_E = 1.4426950408889634
C_WIDTH = D_MODEL - A_WIDTH - B_WIDTH
A_IN = 3 * A_WIDTH + DECAY_LORA + ICL_LORA + GATE_LORA
B_IN = Q_LORA + KV_LORA + QK_ROPE
C_IN = 3 * C_WIDTH
EPS = 1e-6

LANES = 128
SUBLANES = 8
HEAD_PAD = LANES
QP_WIDTH = B_HEADS * HEAD_PAD
KV_IN_PAD = KV_LORA + LANES
P_PAD = A_IN + Q_LORA + KV_IN_PAD + C_IN
VMEM_LIMIT = 48 * 1024 * 1024
BM = 256
F32 = jnp.float32
BF16 = jnp.bfloat16


def _const_spec(shape, nargs):
    zeros = tuple(0 for _ in shape)
    if nargs == 1:
        return pl.BlockSpec(shape, lambda i: zeros)
    if nargs == 2:
        return pl.BlockSpec(shape, lambda i, j: zeros)
    return pl.BlockSpec(shape, lambda i, j, k: zeros)


def _params(n_axes):
    return pltpu.CompilerParams(dimension_semantics=("arbitrary",) * n_axes, vmem_limit_bytes=VMEM_LIMIT)


def _rms_rows(x, g):
    return x * lax.rsqrt(jnp.mean(x * x, axis=-1, keepdims=True) + EPS) * g


def _mm_kernel(x_ref, w_ref, o_ref):
    o_ref[...] = jnp.dot(x_ref[...].astype(BF16), w_ref[0].astype(BF16), preferred_element_type=F32)


def pmm(x, w, layer, bn=1024):
    m, k = x.shape
    n = w.shape[2]
    return pl.pallas_call(
        _mm_kernel,
        out_shape=jax.ShapeDtypeStruct((m, n), F32),
        grid=(n // bn,),
        in_specs=[pl.BlockSpec((m, k), lambda j: (0, 0)), pl.BlockSpec((1, k, bn), lambda j: (layer, 0, j))],
        out_specs=pl.BlockSpec((m, bn), lambda j: (0, j)),
        compiler_params=_params(1),
        name="pmm",
    )(x, w)


def _inproj_kernel(x_ref, mod_ref, g_ref, w_ref, pa_o, pq_o, pkv_o, pc_o):
    sh = mod_ref[0, :, 0:D_MODEL]
    sc = mod_ref[0, :, D_MODEL:2 * D_MODEL]
    h = (_rms_rows(x_ref[0], g_ref[...]) * (1.0 + sc) + sh).astype(BF16)
    o = 0
    for out in (pa_o, pq_o, pkv_o, pc_o):
        n = out.shape[-1]
        out[0] = jnp.dot(h, w_ref[:, o:o + n], preferred_element_type=F32)
        o += n


def inproj(xa, mod, g, w_p):
    bsz, t_all, d = xa.shape
    nblk = t_all // BM
    widths = (A_IN, Q_LORA, KV_IN_PAD, C_IN)
    return pl.pallas_call(
        _inproj_kernel,
        out_shape=[jax.ShapeDtypeStruct((bsz, t_all, n), F32) for n in widths],
        grid=(bsz, nblk),
        in_specs=[pl.BlockSpec((1, BM, d), lambda b, j: (b, j, 0)),
                  pl.BlockSpec((1, 1, 6 * d), lambda b, j: (2 * b + jnp.minimum(j, 1), 0, 0)),
                  _const_spec((1, d), 2), _const_spec((d, P_PAD), 2)],
        out_specs=[pl.BlockSpec((1, BM, n), lambda b, j: (b, j, 0)) for n in widths],
        compiler_params=_params(2),
        name="inproj",
    )(xa, mod, g[None, :], w_p)


def _seg_matrix():
    h = jnp.arange(A_WIDTH) // A_HEAD_DIM
    return (h[:, None] == h[None, :]).astype(BF16)


def _segsum(z, seg):
    hi = z.astype(BF16)
    lo = (z - hi.astype(F32)).astype(BF16)
    return jnp.dot(hi, seg, preferred_element_type=F32) + jnp.dot(lo, seg, preferred_element_type=F32)


RW_TT = 128
CHAIN_PAD = SUBLANES


def _rwkv_prep_kernel(p_ref, pprev_ref, pnext_ref, mu_ref, wlo_ref, wg_ref, bias_ref, ka_ref, rk_ref, seg_ref,
                      r_o, k_o, v_o, df_o, db_o, if_o, ib_o, vk_o, bonus_o, gate_o, pk_s, *, n_ctx, t_all):
    t0 = pl.program_id(1) * RW_TT
    x = p_ref[0]
    first = jnp.logical_or(t0 == 0, t0 == n_ctx)
    last = jnp.logical_or(t0 + RW_TT == n_ctx, t0 + RW_TT == t_all)
    prev_row = jnp.where(first, 0.0, pprev_ref[0, SUBLANES - 1:SUBLANES, :])
    next_row = jnp.where(last, 0.0, pnext_ref[0, 0:1, :])
    rows = lax.broadcasted_iota(jnp.int32, x.shape, 0)
    xp = jnp.where(rows == 0, prev_row, pltpu.roll(x, 1, axis=0))
    xn = jnp.where(rows == RW_TT - 1, next_row, pltpu.roll(x, RW_TT - 1, axis=0))
    xs = x + mu_ref[0:1, :] * (xp - x) + mu_ref[1:2, :] * (xn - x)

    o1, o2, o3 = A_WIDTH, 2 * A_WIDTH, 3 * A_WIDTH
    o5 = o3 + DECAY_LORA + ICL_LORA
    r, k, v = xs[:, :o1], xs[:, o1:o2], xs[:, o2:o3]
    lo2 = xs[:, o3:o5]
    lane = lax.broadcasted_iota(jnp.int32, lo2.shape, 1)
    lo2 = jnp.where(lane < DECAY_LORA, jnp.tanh(lo2), lo2).astype(BF16)
    lo_out = jnp.dot(lo2, wlo_ref[...], preferred_element_type=F32) + bias_ref[...]
    gate = jnp.dot(jax.nn.sigmoid(xs[:, o5:]).astype(BF16), wg_ref[...], preferred_element_type=F32)
    dec_f, dec_b = lo_out[:, :o1], lo_out[:, o1:o2]
    icl_f, icl_b = lo_out[:, o2:o3], lo_out[:, o3:]
    m = 2.0 + (jax.nn.sigmoid(icl_f) + jax.nn.sigmoid(icl_b) - 2.0) * ka_ref[...]
    rkm = r * k * m
    seg = seg_ref[...]
    vk_o[0] = v * _segsum(rkm, seg)
    bonus_o[0] = v * _segsum(rkm * rk_ref[...], seg)
    gate_o[0] = gate
    half = RW_TT // 2
    zero = jnp.zeros((half, LANES), F32)
    for i, (val, out) in enumerate(((r, r_o), (k, k_o), (v, v_o), (dec_f, df_o), (dec_b, db_o),
                                    (icl_f, if_o), (icl_b, ib_o))):
        heads = []
        for cb in range(A_WIDTH // LANES):
            pk_s[i, cb] = val[:, cb * LANES:(cb + 1) * LANES]
            even = pk_s[i, cb, pl.ds(0, half, stride=2), :]
            odd = pk_s[i, cb, pl.ds(1, half, stride=2), :]
            for hh in range(2):
                sl = slice(hh * A_HEAD_DIM, (hh + 1) * A_HEAD_DIM)
                heads.append(jnp.concatenate([even[:, sl], odd[:, sl]], axis=-1))
        heads += [zero] * (CHAIN_PAD - A_HEADS)
        out[0] = jnp.swapaxes(jnp.stack(heads, axis=0), 0, 1)


def rwkv_prep(p_a, n_ctx, tshift_mu, decay_w0, decay_up, icl_a0, icl_up, gate_up, k_a, r_k):
    bsz, t_all, _ = p_a.shape
    assert n_ctx % RW_TT == 0 and t_all % RW_TT == 0
    zeros = jnp.zeros((DECAY_LORA, 2 * A_WIDTH), F32)
    wlo = jnp.concatenate([
        jnp.concatenate([decay_up[0], decay_up[1], zeros], axis=1),
        jnp.concatenate([zeros, icl_up[0], icl_up[1]], axis=1)], axis=0).astype(BF16)
    bias = jnp.concatenate([decay_w0[0], decay_w0[1], icl_a0[0], icl_a0[1]])[None, :]
    nb8 = t_all // SUBLANES
    per8 = RW_TT // SUBLANES
    cm = jax.ShapeDtypeStruct((bsz, t_all // 2, CHAIN_PAD, LANES), F32)
    tm = jax.ShapeDtypeStruct((bsz, t_all, A_WIDTH), F32)
    cm_spec = pl.BlockSpec((1, RW_TT // 2, CHAIN_PAD, LANES), lambda b, j: (b, j, 0, 0))
    tm_spec = pl.BlockSpec((1, RW_TT, A_WIDTH), lambda b, j: (b, j, 0))
    return pl.pallas_call(
        functools.partial(_rwkv_prep_kernel, n_ctx=n_ctx, t_all=t_all),
        out_shape=[cm] * 7 + [tm] * 3,
        grid=(bsz, t_all // RW_TT),
        in_specs=[pl.BlockSpec((1, RW_TT, A_IN), lambda b, j: (b, j, 0)),
                  pl.BlockSpec((1, SUBLANES, A_IN), lambda b, j: (b, jnp.maximum(j * per8 - 1, 0), 0)),
                  pl.BlockSpec((1, SUBLANES, A_IN), lambda b, j: (b, jnp.minimum((j + 1) * per8, nb8 - 1), 0)),
                  _const_spec((2, A_IN), 2), _const_spec((2 * DECAY_LORA, 4 * A_WIDTH), 2),
                  _const_spec((GATE_LORA, A_WIDTH), 2), _const_spec((1, 4 * A_WIDTH), 2),
                  _const_spec((1, A_WIDTH), 2), _const_spec((1, A_WIDTH), 2), _const_spec((A_WIDTH, A_WIDTH), 2)],
        out_specs=[cm_spec] * 7 + [tm_spec] * 3,
        scratch_shapes=[pltpu.VMEM((7, A_WIDTH // LANES, RW_TT, LANES), F32)],
        compiler_params=_params(2),
        name="rwkv_prep",
    )(p_a, p_a, p_a, tshift_mu, wlo, gate_up.astype(BF16), bias, k_a[None, :], r_k.reshape(1, A_WIDTH),
      _seg_matrix())


WKV_TB = 16
WKV_ROWS = WKV_TB // 2
WKV_ACC = 2
WKV_VSPLIT = 2


def _wkv_kernel(rf_ref, rb_ref, kf_ref, kb_ref, vf_ref, vb_ref, df_ref, db_ref, if_ref, ib_ref,
                kk_ref, ka_ref, yf_ref, yb_ref,
                s_ref, a_s, b_s, k_s, rp_s, v_s, y_s, w_s, gp_s, gi_s, g_s, nat_s, *, n_chain):
    @pl.when(pl.program_id(0) == 0)
    def _():
        s_ref[...] = jnp.zeros_like(s_ref)

    lane = lax.broadcasted_iota(jnp.int32, (A_HEAD_DIM, LANES), 1)
    is_fwd = lane < n_chain
    n_batch = n_chain // CHAIN_PAD

    srcs =((rf_ref, rb_ref), (kf_ref, kb_ref), (vf_ref, vb_ref), (if_ref, ib_ref), (df_ref, db_ref))
    q_dec = len(srcs) - 1

    def to_tiles(q, j):
        f_ref, b_ref = srcs[q]
        x = jnp.concatenate([f_ref[b, j] for b in range(n_batch)] + [b_ref[b, j] for b in range(n_batch)],
                            axis=0)
        nat_s[q, 2 * j:2 * j + 2] = x.T.reshape(2, A_HEAD_DIM, LANES)

    def step_tile(q, i):
        return jnp.where(is_fwd, nat_s[q, i], nat_s[q, WKV_TB - 1 - i])

    for j in range(WKV_ROWS):
        to_tiles(q_dec, j)
    g = None
    for i in range(WKV_TB):
        w = jnp.exp(-LOG_DECAY_SCALE * jax.nn.sigmoid(step_tile(q_dec, i)))
        g = w if g is None else g * w
        w_s[i] = w
        gi_s[i] = 1.0 / g
        if i + 1 < WKV_TB:
            gp_s[i + 1] = g
    g_s[...] = g

    kk_gain = kk_ref[...]
    ka = ka_ref[...]
    for ja in range(WKV_ROWS // 2):
        jb = WKV_ROWS - 1 - ja
        for q in range(q_dec):
            to_tiles(q, ja)
            to_tiles(q, jb)
        for i in (2 * ja, 2 * ja + 1, 2 * jb, 2 * jb + 1):
            r, k, v, icl = (step_tile(q, i) for q in range(q_dec))
            icl = jax.nn.sigmoid(icl)
            g_inv = gi_s[i]
            kk = k * kk_gain
            kk = kk * lax.rsqrt(jnp.sum(kk * kk, axis=0, keepdims=True) + 1e-12)
            b = kk * icl
            rp = w_s[i] * r - kk * jnp.sum(b * r, axis=0, keepdims=True)
            v_s[i] = v
            a_s[i] = -kk if i == 0 else -kk * gp_s[i]
            rp_s[i] = rp if i == 0 else rp * gp_s[i]
            b_s[i] = b * g_inv
            k_s[i] = k * (1.0 + (icl - 1.0) * ka) * g_inv

    def step(t, carry):
        rows = A_HEAD_DIM // WKV_VSPLIT
        for part in range(WKV_VSPLIT):
            vs = slice(part * rows, (part + 1) * rows)
            sa_p = [None] * WKV_ACC
            y_p = [None] * WKV_ACC
            for kx in range(A_HEAD_DIM):
                s = s_ref[kx, vs, :]
                pa = s * a_s[t, kx:kx + 1, :]
                py = s * rp_s[t, kx:kx + 1, :]
                i = kx % WKV_ACC
                sa_p[i] = pa if sa_p[i] is None else sa_p[i] + pa
                y_p[i] = py if y_p[i] is None else y_p[i] + py
            sa = functools.reduce(lambda u, w: u + w, sa_p)
            y_s[t, vs, :] = functools.reduce(lambda u, w: u + w, y_p)
            vt = v_s[t, vs, :]
            for kx in range(A_HEAD_DIM):
                s_ref[kx, vs, :] = s_ref[kx, vs, :] + sa * b_s[t, kx:kx + 1, :] + vt * k_s[t, kx:kx + 1, :]
        return carry

    lax.fori_loop(0, WKV_TB, step, 0)

    for kx in range(A_HEAD_DIM):
        s_ref[kx] = s_ref[kx] * g_s[kx:kx + 1, :]

    for j in range(WKV_ROWS):
        y = jnp.concatenate([jnp.where(is_fwd, y_s[i], y_s[WKV_TB - 1 - i]) for i in (2 * j, 2 * j + 1)], axis=0)
        yt = y.T
        for b in range(n_batch):
            yf_ref[b, j] = yt[b * CHAIN_PAD:(b + 1) * CHAIN_PAD]
            yb_ref[b, j] = yt[n_chain + b * CHAIN_PAD:n_chain + (b + 1) * CHAIN_PAD]


def wkv_scan(r, k, v, dec_f, dec_b, icl_f, icl_b, k_k, k_a, n_ctx):
    bsz, t_half, _, _ = r.shape
    n_chain = bsz * CHAIN_PAD
    t_all, kd = 2 * t_half, A_HEAD_DIM
    assert n_ctx % WKV_TB == 0 and t_all % WKV_TB == 0 and 2 * n_chain == LANES
    nb_ctx = n_ctx // WKV_TB
    nb = t_all // WKV_TB

    def bwd_block(j):
        return jnp.where(j < nb_ctx, nb_ctx - 1 - j, nb + nb_ctx - 1 - j)

    f_spec = pl.BlockSpec((bsz, WKV_ROWS, CHAIN_PAD, LANES), lambda j: (0, j, 0, 0))
    b_spec = pl.BlockSpec((bsz, WKV_ROWS, CHAIN_PAD, LANES), lambda j: (0, bwd_block(j), 0, 0))
    tile_spec = pl.BlockSpec((kd, LANES), lambda j: (0, 0))

    def lane_tile(p):
        t = jnp.pad(p.reshape(A_HEADS, kd).T, ((0, 0), (0, CHAIN_PAD - A_HEADS)))
        return jnp.tile(t, (1, LANES // CHAIN_PAD))

    seq = pltpu.VMEM((WKV_TB, kd, LANES), F32)
    out = jax.ShapeDtypeStruct(r.shape, F32)
    yf, yb = pl.pallas_call(
        functools.partial(_wkv_kernel, n_chain=n_chain),
        out_shape=[out, out],
        grid=(nb,),
        in_specs=[f_spec, b_spec] * 5 + [tile_spec, tile_spec],
        out_specs=[f_spec, b_spec],
        scratch_shapes=[pltpu.VMEM((kd, kd, LANES), F32)] + [seq] * 9 + [
            pltpu.VMEM((kd, LANES), F32), pltpu.VMEM((5, WKV_TB, kd, LANES), F32)],
        compiler_params=_params(1),
        name="wkv_scan",
    )(r, r, k, k, v, v, dec_f, dec_b, icl_f, icl_b, lane_tile(k_k), lane_tile(k_a))
    return yf, yb


def _rwkv_post_kernel(yf_ref, yb_ref, vk_ref, bonus_ref, gate_ref, g_ref, b_ref, seg_ref, o_ref, y_s):
    half = RW_TT // 2
    yh = jnp.swapaxes(yf_ref[0] + yb_ref[0], 0, 1)
    for cb in range(A_WIDTH // LANES):
        y0 = yh[2 * cb]
        y1 = yh[2 * cb + 1]
        y_s[cb, pl.ds(0, half, stride=2), :] = jnp.concatenate([y0[:, :A_HEAD_DIM], y1[:, :A_HEAD_DIM]], axis=-1)
        y_s[cb, pl.ds(1, half, stride=2), :] = jnp.concatenate([y0[:, A_HEAD_DIM:], y1[:, A_HEAD_DIM:]], axis=-1)
    y = jnp.concatenate([y_s[cb] for cb in range(A_WIDTH // LANES)], axis=-1) + vk_ref[0]
    seg = seg_ref[...]
    mu = _segsum(y, seg) * (1.0 / A_HEAD_DIM)
    d = y - mu
    var = _segsum(d * d, seg) * (1.0 / A_HEAD_DIM)
    yn = d * lax.rsqrt(var + GN_EPS) * g_ref[...] + b_ref[...]
    o_ref[0] = ((yn + bonus_ref[0]) * gate_ref[0]).astype(o_ref.dtype)


def rwkv_post(yf, yb, vk, bonus, gate, lnx_g, lnx_b, out_dtype=BF16):
    bsz, t_all, _ = vk.shape
    cm_spec = pl.BlockSpec((1, RW_TT // 2, CHAIN_PAD, LANES), lambda b, j: (b, j, 0, 0))
    tm_spec = pl.BlockSpec((1, RW_TT, A_WIDTH), lambda b, j: (b, j, 0))
    return pl.pallas_call(
        _rwkv_post_kernel,
        out_shape=jax.ShapeDtypeStruct((bsz, t_all, A_WIDTH), out_dtype),
        grid=(bsz, t_all // RW_TT),
        in_specs=[cm_spec, cm_spec, tm_spec, tm_spec, tm_spec, _const_spec((1, A_WIDTH), 2),
                  _const_spec((1, A_WIDTH), 2), _const_spec((A_WIDTH, A_WIDTH), 2)],
        out_specs=tm_spec,
        scratch_shapes=[pltpu.VMEM((A_WIDTH // LANES, RW_TT, LANES), F32)],
        compiler_params=_params(2),
        name="rwkv_post",
    )(yf, yb, vk, bonus, gate, lnx_g[None, :], lnx_b[None, :], _seg_matrix())


def rwkv_group(p_a, n_ctx, tshift_mu, decay_w0, decay_up, icl_a0, icl_up, gate_up, k_k, k_a, r_k, lnx_g, lnx_b,
               out_dtype=BF16):
    r, k, v, dec_f, dec_b, icl_f, icl_b, vk, bonus, gate = rwkv_prep(
        p_a, n_ctx, tshift_mu, decay_w0, decay_up, icl_a0, icl_up, gate_up, k_a, r_k)
    yf, yb = wkv_scan(r, k, v, dec_f, dec_b, icl_f, icl_b, k_k, k_a, n_ctx)
    return rwkv_post(yf, yb, vk, bonus, gate, lnx_g, lnx_b, out_dtype)


def _rope_lanes(z, cos, sin_lo, sin_hi):
    return (z * cos + pltpu.roll(z, LANES - SUBLANES, axis=1) * sin_lo + pltpu.roll(z, SUBLANES, axis=1) * sin_hi)


def _mla_q_kernel(pq_ref, gq_ref, w_ref, m_ref, gain_ref, cos_ref, slo_ref, shi_ref, q_o):
    h = _rms_rows(pq_ref[0], gq_ref[...]).astype(BF16)
    q = jnp.dot(h, w_ref[...], preferred_element_type=F32)
    ms = jnp.dot((q * q).astype(BF16), m_ref[...], preferred_element_type=F32)
    qh = q * lax.rsqrt(ms + EPS) * gain_ref[...]
    cos, slo, shi = cos_ref[...], slo_ref[...], shi_ref[...]
    tiles = [_rope_lanes(qh[:, i * HEAD_PAD:(i + 1) * HEAD_PAD], cos, slo, shi) for i in range(B_HEADS)]
    q_o[0] = (jnp.concatenate(tiles, axis=-1) * (ATTN_SCALE * LOG2_E)).astype(BF16)


def _mla_kv_kernel(pkv_ref, gkv_ref, wk_ref, wv_ref, m_ref, gain_ref, gr_ref, cos_ref, slo_ref, shi_ref,
                   place_ref, k_o, v_o):
    x = pkv_ref[0]
    h = _rms_rows(x[:, :KV_LORA], gkv_ref[...]).astype(BF16)
    kn = jnp.dot(h, wk_ref[...], preferred_element_type=F32)
    v = jnp.dot(h, wv_ref[...], preferred_element_type=F32)
    lane = lax.broadcasted_iota(jnp.int32, (v.shape[0], LANES), 1)
    tiles = []
    for pair in range(B_HEADS // 2):
        vp = v[:, pair * LANES:(pair + 1) * LANES]
        tiles += [jnp.where(lane < V_HEAD, vp, 1.0), jnp.where(lane < V_HEAD, 1.0, vp)]
    v_o[0] = jnp.concatenate(tiles, axis=-1).astype(BF16)
    ms = jnp.dot((kn * kn).astype(BF16), m_ref[...], preferred_element_type=F32)
    knh = kn * lax.rsqrt(ms + EPS) * gain_ref[...]
    kr = x[:, KV_LORA:]
    kr = kr * lax.rsqrt(jnp.sum(kr * kr, axis=-1, keepdims=True) * (1.0 / QK_ROPE) + EPS) * gr_ref[...]
    kr = _rope_lanes(kr, cos_ref[...], slo_ref[...], shi_ref[...]).astype(BF16)
    k_o[0] = (knh + jnp.dot(kr, place_ref[...], preferred_element_type=F32)).astype(BF16)


def _head_pattern(nope, rope):
    tile = jnp.concatenate([nope, rope, jnp.zeros((HEAD_PAD - QK_NOPE - QK_ROPE,), F32)])
    return jnp.tile(tile, B_HEADS)[None, :]


def _seg_mean_matrix(with_rope):
    i = jnp.arange(QP_WIDTH)
    head, off = i // HEAD_PAD, i % HEAD_PAD
    nope = off < QK_NOPE
    rope = jnp.logical_and(off >= QK_NOPE, off < QK_NOPE + QK_ROPE)
    same = head[:, None] == head[None, :]
    m = jnp.where(same & nope[:, None] & nope[None, :], 1.0 / QK_NOPE, 0.0)
    if with_rope:
        m = m + jnp.where(same & rope[:, None] & rope[None, :], 1.0 / QK_ROPE, 0.0)
    return m.astype(BF16)


def _rope_tables(n, n_ctx):
    t = jnp.arange(n, dtype=jnp.int32)
    r_pos = (t // GRID_W).astype(F32)
    c_pos = (t % GRID_W).astype(F32)
    axis_dim = QK_ROPE // 2
    inv_freq = 1.0 / (ROPE_BASE ** (jnp.arange(0, axis_dim, 2, dtype=F32) / axis_dim))
    ang_r = r_pos[:, None] * inv_freq
    ang_c = c_pos[:, None] * inv_freq
    half = axis_dim // 2
    z = jnp.zeros((n, half), F32)
    cos = jnp.concatenate([jnp.cos(ang_r)] * 2 + [jnp.cos(ang_c)] * 2, axis=1)
    s_lo = jnp.concatenate([-jnp.sin(ang_r), z, -jnp.sin(ang_c), z], axis=1)
    s_hi = jnp.concatenate([z, jnp.sin(ang_r), z, jnp.sin(ang_c)], axis=1)

    def widen(tab, fill):
        tab = jnp.concatenate([jnp.full((n_ctx, QK_ROPE), fill, F32), tab], axis=0)
        return jnp.concatenate([jnp.full((n_ctx + n, QK_NOPE), fill, F32), tab,
                                jnp.full((n_ctx + n, HEAD_PAD - QK_NOPE - QK_ROPE), fill, F32)], axis=1)

    return widen(cos, 1.0), widen(s_lo, 0.0), widen(s_hi, 0.0)


def mla_qkv(p_q, p_kv, tabs, q_norm_g, kv_norm_g, w_uq, w_ukv, q_nope_g, k_nope_g, q_rope_g, k_rope_g):
    bsz, t_all, _ = p_q.shape
    nblk = t_all // BM
    dq = QK_NOPE + QK_ROPE
    w_q = jnp.pad(w_uq.reshape(Q_LORA, B_HEADS, dq), ((0, 0), (0, 0), (0, HEAD_PAD - dq))).reshape(
        Q_LORA, QP_WIDTH).astype(BF16)
    w4 = w_ukv.reshape(KV_LORA, B_HEADS, QK_NOPE + V_HEAD)
    w_k = jnp.pad(w4[:, :, :QK_NOPE], ((0, 0), (0, 0), (0, HEAD_PAD - QK_NOPE))).reshape(
        KV_LORA, QP_WIDTH).astype(BF16)
    w_v = w4[:, :, QK_NOPE:].reshape(KV_LORA, B_WIDTH).astype(BF16)
    zero_r = jnp.zeros((QK_ROPE,), F32)
    i = jnp.arange(LANES)
    rope_lane = jnp.logical_and(i >= QK_NOPE, i < QK_NOPE + QK_ROPE)
    place = jnp.where(rope_lane[:, None] & (i[:, None] == (jnp.arange(QP_WIDTH) % HEAD_PAD)[None, :]),
                      1.0, 0.0).astype(BF16)
    gr = jnp.concatenate([jnp.zeros((QK_NOPE,), F32), k_rope_g, jnp.zeros((LANES - QK_NOPE - QK_ROPE,), F32)])

    row_spec = lambda n: pl.BlockSpec((1, BM, n), lambda j, b: (b, j, 0))
    tab_spec = pl.BlockSpec((BM, LANES), lambda j, b: (j, 0))
    c = lambda shape: _const_spec(shape, 2)
    q = pl.pallas_call(
        _mla_q_kernel,
        out_shape=jax.ShapeDtypeStruct((bsz, t_all, QP_WIDTH), BF16),
        grid=(nblk, bsz),
        in_specs=[row_spec(Q_LORA), c((1, Q_LORA)), c((Q_LORA, QP_WIDTH)), c((QP_WIDTH, QP_WIDTH)),
                  c((1, QP_WIDTH)), tab_spec, tab_spec, tab_spec],
        out_specs=row_spec(QP_WIDTH),
        compiler_params=_params(2),
        name="mla_q",
    )(p_q, q_norm_g[None, :], w_q, _seg_mean_matrix(True), _head_pattern(q_nope_g, q_rope_g), *tabs)
    k, v = pl.pallas_call(
        _mla_kv_kernel,
        out_shape=[jax.ShapeDtypeStruct((bsz, t_all, QP_WIDTH), BF16),
                   jax.ShapeDtypeStruct((bsz, t_all, QP_WIDTH), BF16)],
        grid=(nblk, bsz),
        in_specs=[row_spec(KV_IN_PAD), c((1, KV_LORA)), c((KV_LORA, QP_WIDTH)), c((KV_LORA, B_WIDTH)),
                  c((QP_WIDTH, QP_WIDTH)), c((1, QP_WIDTH)), c((1, LANES)), tab_spec, tab_spec, tab_spec,
                  c((LANES, QP_WIDTH))],
        out_specs=[row_spec(QP_WIDTH), row_spec(QP_WIDTH)],
        compiler_params=_params(2),
        name="mla_kv",
    )(p_kv, kv_norm_g[None, :], w_k, w_v, _seg_mean_matrix(False), _head_pattern(k_nope_g, zero_r),
      gr[None, :], *tabs, place)
    return q, k, v


ATT_BQ = 256
HEADS_PER_STEP = 2


def _attn_kernel(q_ref, k_ref, v_ref, o_ref):
    scores = []
    for h in range(HEADS_PER_STEP):
        q = q_ref[0, :, h * HEAD_PAD:(h + 1) * HEAD_PAD]
        k = k_ref[0, :, h * HEAD_PAD:(h + 1) * HEAD_PAD]
        scores.append(lax.dot_general(q, k, (((1,), (1,)), ((), ())), preferred_element_type=F32))
    outs = []
    for h in range(HEADS_PER_STEP):
        s = scores[h]
        p = jnp.exp2(s - jnp.max(s, axis=-1, keepdims=True)).astype(BF16)
        outs.append(jnp.dot(p, v_ref[0, :, h * HEAD_PAD:(h + 1) * HEAD_PAD], preferred_element_type=F32))
    lane = lax.broadcasted_iota(jnp.int32, outs[0].shape, 1)
    first = lane < V_HEAD
    num = jnp.where(first, outs[0], outs[1])
    den = pltpu.roll(jnp.where(first, outs[1], outs[0]), V_HEAD, axis=1)
    o_ref[0] = (num / den).astype(o_ref.dtype)


def attention(q, k, v, q_row0, nq, nk):
    bsz = q.shape[0]
    bq = min(ATT_BQ, nq)
    qb0 = q_row0 // bq
    pair = HEADS_PER_STEP * HEAD_PAD
    return pl.pallas_call(
        _attn_kernel,
        out_shape=jax.ShapeDtypeStruct((bsz, nq, B_WIDTH), BF16),
        grid=(bsz, B_HEADS // HEADS_PER_STEP, nq // bq),
        in_specs=[pl.BlockSpec((1, bq, pair), lambda b, hp, i: (b, qb0 + i, hp)),
                  pl.BlockSpec((1, nk, pair), lambda b, hp, i: (b, 0, hp)),
                  pl.BlockSpec((1, nk, pair), lambda b, hp, i: (b, 0, hp))],
        out_specs=pl.BlockSpec((1, bq, LANES), lambda b, hp, i: (b, i, hp)),
        compiler_params=_params(3),
        name="mla_attention",
    )(q, k, v)


def _outproj_kernel(ya_ref, ybc_ref, ybl_ref, pc_ref, pcp_ref, pcn_ref, x_ref, mod_ref, cw_ref, w_ref, o_ref, *,
                    row_off, n_ctx, t_all):
    j = pl.program_id(1) + row_off
    t0 = j * BM
    first = jnp.logical_or(t0 == 0, t0 == n_ctx)
    last = jnp.logical_or(t0 + BM == n_ctx, t0 + BM == t_all)
    o1, o2 = C_WIDTH, 2 * C_WIDTH
    pc = pc_ref[0]
    u = pc[:, o1:o2] * pc[:, o2:]
    pcp = pcp_ref[0, SUBLANES - 1:SUBLANES, :]
    pcn = pcn_ref[0, 0:1, :]
    u_prev = jnp.where(first, 0.0, pcp[:, o1:o2] * pcp[:, o2:])
    u_next = jnp.where(last, 0.0, pcn[:, o1:o2] * pcn[:, o2:])
    rows = lax.broadcasted_iota(jnp.int32, u.shape, 0)
    up = jnp.where(rows == 0, u_prev, pltpu.roll(u, 1, axis=0))
    un = jnp.where(rows == BM - 1, u_next, pltpu.roll(u, BM - 1, axis=0))
    yc = pc[:, :o1] * (cw_ref[0:1, :] * up + cw_ref[1:2, :] * u + cw_ref[2:3, :] * un)
    yb = jnp.where(j == 0, ybc_ref[0], ybl_ref[0]) if row_off == 0 else ybl_ref[0]
    acc = jnp.dot(ya_ref[0], w_ref[0:A_WIDTH, :], preferred_element_type=F32)
    acc += jnp.dot(yb, w_ref[A_WIDTH:A_WIDTH + B_WIDTH, :], preferred_element_type=F32)
    acc += jnp.dot(yc.astype(BF16), w_ref[A_WIDTH + B_WIDTH:, :], preferred_element_type=F32)
    o_ref[0] = x_ref[0] + mod_ref[0, :, 2 * D_MODEL:3 * D_MODEL] * acc


def outproj(ya, yb_ctx, yb_lat, p_c, xa, mod, conv_w, w_out, n_ctx, with_ctx):
    bsz, t_all, d = xa.shape
    row_off = 0 if with_ctx else n_ctx // BM
    nblk = t_all // BM - row_off
    nb8 = t_all // SUBLANES
    per8 = BM // SUBLANES
    rows = lambda n: pl.BlockSpec((1, BM, n), lambda b, j: (b, j + row_off, 0))
    lat_off = n_ctx // BM
    return pl.pallas_call(
        functools.partial(_outproj_kernel, row_off=row_off, n_ctx=n_ctx, t_all=t_all),
        out_shape=jax.ShapeDtypeStruct((bsz, nblk * BM, d), F32),
        grid=(bsz, nblk),
        in_specs=[rows(A_WIDTH),
                  pl.BlockSpec((1, BM, B_WIDTH), lambda b, j: (b, 0, 0)),
                  pl.BlockSpec((1, BM, B_WIDTH), lambda b, j: (b, jnp.maximum(j + row_off - lat_off, 0), 0)),
                  rows(C_IN),
                  pl.BlockSpec((1, SUBLANES, C_IN), lambda b, j: (b, jnp.maximum((j + row_off) * per8 - 1, 0), 0)),
                  pl.BlockSpec((1, SUBLANES, C_IN),
                               lambda b, j: (b, jnp.minimum((j + row_off + 1) * per8, nb8 - 1), 0)),
                  rows(d),
                  pl.BlockSpec((1, 1, 6 * d), lambda b, j: (2 * b + jnp.minimum(j + row_off, 1), 0, 0)),
                  _const_spec((3, C_WIDTH), 2), _const_spec((d, d), 2)],
        out_specs=pl.BlockSpec((1, BM, d), lambda b, j: (b, j, 0)),
        compiler_params=_params(2),
        name="outproj",
    )(ya, yb_ctx, yb_lat, p_c, p_c, p_c, xa, mod, conv_w, w_out)


FFN_CHUNKS = 2


def _ffn_in_kernel(x_ref, mod_ref, g_ref, w_ref, o_ref):
    sh = mod_ref[0, :, 3 * D_MODEL:4 * D_MODEL]
    sc = mod_ref[0, :, 4 * D_MODEL:5 * D_MODEL]
    h = (_rms_rows(x_ref[0], g_ref[...]) * (1.0 + sc) + sh).astype(BF16)
    d_ff = o_ref.shape[-1]
    cw = d_ff // FFN_CHUNKS
    for c in range(FFN_CHUNKS):
        gate = jnp.dot(h, w_ref[:, c * cw:(c + 1) * cw], preferred_element_type=F32)
        up = jnp.dot(h, w_ref[:, d_ff + c * cw:d_ff + (c + 1) * cw], preferred_element_type=F32)
        o_ref[0, :, c * cw:(c + 1) * cw] = (gate * jax.nn.sigmoid(gate) * up).astype(o_ref.dtype)


def _ffn_out_kernel(a_ref, x_ref, mod_ref, w_ref, o_ref):
    acc = jnp.dot(a_ref[0], w_ref[...], preferred_element_type=F32)
    o_ref[0] = x_ref[0] + mod_ref[0, :, 5 * D_MODEL:6 * D_MODEL] * acc


def ffn(x1, mod, g, w_fi, w_fo, lat_only):
    bsz, t, d = x1.shape
    d_ff = w_fo.shape[0]
    seg = 1 if lat_only else 0
    rows = lambda n: pl.BlockSpec((1, BM, n), lambda b, j: (b, j, 0))
    mod_spec = pl.BlockSpec((1, 1, 6 * d), lambda b, j: (2 * b + jnp.minimum(j + seg, 1), 0, 0))
    act = pl.pallas_call(
        _ffn_in_kernel,
        out_shape=jax.ShapeDtypeStruct((bsz, t, d_ff), BF16),
        grid=(bsz, t // BM),
        in_specs=[rows(d), mod_spec, _const_spec((1, d), 2), _const_spec((d, 2 * d_ff), 2)],
        out_specs=rows(d_ff),
        compiler_params=_params(2),
        name="ffn_in",
    )(x1, mod, g[None, :], w_fi)
    return pl.pallas_call(
        _ffn_out_kernel,
        out_shape=jax.ShapeDtypeStruct((bsz, t, d), F32),
        grid=(bsz, t // BM),
        in_specs=[rows(d_ff), rows(d), mod_spec, _const_spec((d_ff, d), 2)],
        out_specs=rows(d),
        compiler_params=_params(2),
        name="ffn_out",
    )(act, x1, mod, w_fo)


def kernel(x, c, ctx, c_ctx, ada_w, ada_b, norm1_g, norm2_g, w_in, tshift_mu, decay_w0, decay_up, icl_a0, icl_up, gate_up, k_k, k_a, r_k, lnx_g, lnx_b, q_norm_g, kv_norm_g, w_uq, w_ukv, q_nope_g, k_nope_g, q_rope_g, k_rope_g, conv_w, w_out, w_ffn_in, w_ffn_out):
    bsz, n, d = x.shape
    n_ctx = ctx.shape[1]
    depth = ada_w.shape[0]
    assert n_ctx == BM and n % BM == 0 and d == D_MODEL
    tabs = _rope_tables(n, n_ctx)

    xa = jnp.concatenate([ctx, x], axis=1)
    cond = jnp.concatenate([c, c_ctx[None, :], jnp.zeros((2 * SUBLANES - bsz - 1, d), c.dtype)], axis=0)
    silu_cond = cond * jax.nn.sigmoid(cond)
    o_kv = A_IN + Q_LORA + KV_LORA
    for l in range(depth):
        last = l == depth - 1
        ada = pmm(silu_cond, ada_w, l) + ada_b[l]
        mod = jnp.stack([jnp.broadcast_to(ada[bsz], (bsz, 6 * d)), ada[:bsz]], axis=1).reshape(2 * bsz, 1, 6 * d)

        wl = w_in[l]
        w_p = jnp.concatenate(
            [wl[:, :o_kv], jnp.zeros((d, QK_NOPE), F32), wl[:, o_kv:A_IN + B_IN],
             jnp.zeros((d, LANES - QK_NOPE - QK_ROPE), F32), wl[:, A_IN + B_IN:]], axis=1).astype(BF16)
        p_a, p_q, p_kv, p_c = inproj(xa, mod, norm1_g[l], w_p)

        ya = rwkv_group(p_a, n_ctx, tshift_mu[l], decay_w0[l], decay_up[l], icl_a0[l], icl_up[l], gate_up[l],
                        k_k[l], k_a[l], r_k[l], lnx_g[l], lnx_b[l])
        q, k, v = mla_qkv(p_q, p_kv, tabs, q_norm_g[l], kv_norm_g[l], w_uq[l], w_ukv[l], q_nope_g[l],
                          k_nope_g[l], q_rope_g[l], k_rope_g[l])
        yb_lat = attention(q, k, v, n_ctx, n, n_ctx + n)
        yb_ctx = yb_lat if last else attention(q, k, v, 0, n_ctx, n_ctx)

        x1 = outproj(ya, yb_ctx, yb_lat, p_c, xa, mod, conv_w[l], w_out[l].astype(BF16), n_ctx, not last)
        xa = ffn(x1, mod, norm2_g[l], w_ffn_in[l].astype(BF16), w_ffn_out[l].astype(BF16), last)
    return xa
```

```python
import functools

import jax
import jax.numpy as jnp
from jax import lax
from jax.experimental import pallas as pl
from jax.experimental.pallas import tpu as pltpu

D_MODEL = 1024
GRID_W = 64
A_HEADS = 6
A_HEAD_DIM = 64
A_WIDTH = A_HEADS * A_HEAD_DIM
DECAY_LORA = 64
ICL_LORA = 64
GATE_LORA = 128
LOG_DECAY_SCALE = 0.606531
GN_EPS = A_HEAD_DIM * 1e-5
B_HEADS = 6
QK_NOPE = 64
QK_ROPE = 32
V_HEAD = 64
B_WIDTH = B_HEADS * V_HEAD
Q_LORA = 768
KV_LORA = 256
ROPE_BASE = 10000.0
ATTN_SCALE = (QK_NOPE + QK_ROPE) ** -0.5
LOG2_E = 1.4426950408889634
C_WIDTH = D_MODEL - A_WIDTH - B_WIDTH
A_IN = 3 * A_WIDTH + DECAY_LORA + ICL_LORA + GATE_LORA
B_IN = Q_LORA + KV_LORA + QK_ROPE
C_IN = 3 * C_WIDTH
EPS = 1e-6

LANES = 128
SUBLANES = 8
HEAD_PAD = LANES
QP_WIDTH = B_HEADS * HEAD_PAD
KV_IN_PAD = KV_LORA + LANES
P_PAD = A_IN + Q_LORA + KV_IN_PAD + C_IN
VMEM_LIMIT = 48 * 1024 * 1024
BM = 256
F32 = jnp.float32
BF16 = jnp.bfloat16


def _const_spec(shape, nargs):
    zeros = tuple(0 for _ in shape)
    if nargs == 1:
        return pl.BlockSpec(shape, lambda i: zeros)
    if nargs == 2:
        return pl.BlockSpec(shape, lambda i, j: zeros)
    return pl.BlockSpec(shape, lambda i, j, k: zeros)


def _params(n_axes):
    return pltpu.CompilerParams(dimension_semantics=("arbitrary",) * n_axes, vmem_limit_bytes=VMEM_LIMIT)


def _rms_rows(x, g):
    return x * lax.rsqrt(jnp.mean(x * x, axis=-1, keepdims=True) + EPS) * g


def _mm_kernel(x_ref, w_ref, o_ref):
    o_ref[...] = jnp.dot(x_ref[...].astype(BF16), w_ref[0].astype(BF16), preferred_element_type=F32)


def pmm(x, w, layer, bn=1024):
    m, k = x.shape
    n = w.shape[2]
    return pl.pallas_call(
        _mm_kernel,
        out_shape=jax.ShapeDtypeStruct((m, n), F32),
        grid=(n // bn,),
        in_specs=[pl.BlockSpec((m, k), lambda j: (0, 0)), pl.BlockSpec((1, k, bn), lambda j: (layer, 0, j))],
        out_specs=pl.BlockSpec((m, bn), lambda j: (0, j)),
        compiler_params=_params(1),
        name="pmm",
    )(x, w)


def _seg_matrix():
    h = jnp.arange(A_WIDTH) // A_HEAD_DIM
    return (h[:, None] == h[None, :]).astype(BF16)


def _segsum(z, seg):
    hi = z.astype(BF16)
    lo = (z - hi.astype(F32)).astype(BF16)
    return jnp.dot(hi, seg, preferred_element_type=F32) + jnp.dot(lo, seg, preferred_element_type=F32)


RW_TT = 128
CHAIN_PAD = SUBLANES


def _rwkv_prep(x, prev_row, next_row, mu_ref, wlo_ref, wg_ref, bias_ref, ka_ref, rk_ref, seg_ref,
               r_o, k_o, v_o, df_o, db_o, if_o, ib_o, vk_o, bonus_o, gate_o, pk_s):
    tt = x.shape[0]
    rows = lax.broadcasted_iota(jnp.int32, x.shape, 0)
    xp = jnp.where(rows == 0, prev_row, pltpu.roll(x, 1, axis=0))
    xn = jnp.where(rows == tt - 1, next_row, pltpu.roll(x, tt - 1, axis=0))
    xs = x + mu_ref[0:1, :] * (xp - x) + mu_ref[1:2, :] * (xn - x)

    o1, o2, o3 = A_WIDTH, 2 * A_WIDTH, 3 * A_WIDTH
    o5 = o3 + DECAY_LORA + ICL_LORA
    r, k, v = xs[:, :o1], xs[:, o1:o2], xs[:, o2:o3]
    lo2 = xs[:, o3:o5]
    lane = lax.broadcasted_iota(jnp.int32, lo2.shape, 1)
    lo2 = jnp.where(lane < DECAY_LORA, jnp.tanh(lo2), lo2).astype(BF16)
    lo_out = jnp.dot(lo2, wlo_ref[...], preferred_element_type=F32) + bias_ref[...]
    gate = jnp.dot(jax.nn.sigmoid(xs[:, o5:]).astype(BF16), wg_ref[...], preferred_element_type=F32)
    dec_f, dec_b = lo_out[:, :o1], lo_out[:, o1:o2]
    icl_f, icl_b = lo_out[:, o2:o3], lo_out[:, o3:]
    m = 2.0 + (jax.nn.sigmoid(icl_f) + jax.nn.sigmoid(icl_b) - 2.0) * ka_ref[...]
    rkm = r * k * m
    seg = seg_ref[...]
    vk_o[0] = v * _segsum(rkm, seg)
    bonus_o[0] = v * _segsum(rkm * rk_ref[...], seg)
    gate_o[0] = gate
    half = tt // 2
    zero = jnp.zeros((half, LANES), F32)
    for i, (val, out) in enumerate(((r, r_o), (k, k_o), (v, v_o), (dec_f, df_o), (dec_b, db_o),
                                    (icl_f, if_o), (icl_b, ib_o))):
        heads = []
        for cb in range(A_WIDTH // LANES):
            pk_s[i, cb] = val[:, cb * LANES:(cb + 1) * LANES]
            even = pk_s[i, cb, pl.ds(0, half, stride=2), :]
            odd = pk_s[i, cb, pl.ds(1, half, stride=2), :]
            for hh in range(2):
                sl = slice(hh * A_HEAD_DIM, (hh + 1) * A_HEAD_DIM)
                heads.append(jnp.concatenate([even[:, sl], odd[:, sl]], axis=-1))
        heads += [zero] * (CHAIN_PAD - A_HEADS)
        out[0] = jnp.swapaxes(jnp.stack(heads, axis=0), 0, 1)


def _inproj_kernel(x_ref, xprev_ref, xnext_ref, mod_ref, g_ref, w_ref, *rest, n_ctx, t_all):
    prep_consts, (pq_o, pkv_o, pc_o), prep_outs = rest[:7], rest[7:10], rest[10:]
    t0 = pl.program_id(1) * BM
    first = jnp.logical_or(t0 == 0, t0 == n_ctx)
    last = jnp.logical_or(t0 + BM == n_ctx, t0 + BM == t_all)
    sh = mod_ref[0, :, 0:D_MODEL]
    sc = mod_ref[0, :, D_MODEL:2 * D_MODEL]
    x_ext = jnp.concatenate([xprev_ref[0], x_ref[0], xnext_ref[0]], axis=0)
    h_ext = (_rms_rows(x_ext, g_ref[...]) * (1.0 + sc) + sh).astype(BF16)
    pa = jnp.dot(h_ext, w_ref[:, :A_IN], preferred_element_type=F32)
    prev_row = jnp.where(first, 0.0, pa[SUBLANES - 1:SUBLANES])
    next_row = jnp.where(last, 0.0, pa[SUBLANES + BM:SUBLANES + BM + 1])
    _rwkv_prep(pa[SUBLANES:SUBLANES + BM], prev_row, next_row, *prep_consts, *prep_outs)
    h = h_ext[SUBLANES:SUBLANES + BM]
    o = A_IN
    for out in (pq_o, pkv_o, pc_o):
        n = out.shape[-1]
        out[0] = jnp.dot(h, w_ref[:, o:o + n], preferred_element_type=F32)
        o += n


def inproj(xa, mod, g, w_p, n_ctx, tshift_mu, decay_w0, decay_up, icl_a0, icl_up, gate_up, k_a, r_k):
    bsz, t_all, d = xa.shape
    nblk = t_all // BM
    zeros = jnp.zeros((DECAY_LORA, 2 * A_WIDTH), F32)
    wlo = jnp.concatenate([
        jnp.concatenate([decay_up[0], decay_up[1], zeros], axis=1),
        jnp.concatenate([zeros, icl_up[0], icl_up[1]], axis=1)], axis=0).astype(BF16)
    bias = jnp.concatenate([decay_w0[0], decay_w0[1], icl_a0[0], icl_a0[1]])[None, :]
    nb8 = t_all // SUBLANES
    per8 = BM // SUBLANES
    widths = (Q_LORA, KV_IN_PAD, C_IN)
    cm = jax.ShapeDtypeStruct((bsz, t_all // 2, CHAIN_PAD, LANES), F32)
    tm = jax.ShapeDtypeStruct((bsz, t_all, A_WIDTH), F32)
    cm_spec = pl.BlockSpec((1, BM // 2, CHAIN_PAD, LANES), lambda b, j: (b, j, 0, 0))
    rows = lambda n: pl.BlockSpec((1, BM, n), lambda b, j: (b, j, 0))
    return pl.pallas_call(
        functools.partial(_inproj_kernel, n_ctx=n_ctx, t_all=t_all),
        out_shape=[jax.ShapeDtypeStruct((bsz, t_all, n), F32) for n in widths] + [cm] * 7 + [tm] * 3,
        grid=(bsz, nblk),
        in_specs=[rows(d),
                  pl.BlockSpec((1, SUBLANES, d), lambda b, j: (b, jnp.maximum(j * per8 - 1, 0), 0)),
                  pl.BlockSpec((1, SUBLANES, d), lambda b, j: (b, jnp.minimum((j + 1) * per8, nb8 - 1), 0)),
                  pl.BlockSpec((1, 1, 6 * d), lambda b, j: (2 * b + jnp.minimum(j, 1), 0, 0)),
                  _const_spec((1, d), 2), _const_spec((d, P_PAD), 2),
                  _const_spec((2, A_IN), 2), _const_spec((2 * DECAY_LORA, 4 * A_WIDTH), 2),
                  _const_spec((GATE_LORA, A_WIDTH), 2), _const_spec((1, 4 * A_WIDTH), 2),
                  _const_spec((1, A_WIDTH), 2), _const_spec((1, A_WIDTH), 2), _const_spec((A_WIDTH, A_WIDTH), 2)],
        out_specs=[rows(n) for n in widths] + [cm_spec] * 7 + [rows(A_WIDTH)] * 3,
        scratch_shapes=[pltpu.VMEM((7, A_WIDTH // LANES, BM, LANES), F32)],
        compiler_params=_params(2),
        name="inproj",
    )(xa, xa, xa, mod, g[None, :], w_p, tshift_mu, wlo, gate_up.astype(BF16), bias, k_a[None, :],
      r_k.reshape(1, A_WIDTH), _seg_matrix())


WKV_TB = 16
WKV_ROWS = WKV_TB // 2
WKV_ACC = 2
WKV_VSPLIT = 2


def _wkv_kernel(rf_ref, rb_ref, kf_ref, kb_ref, vf_ref, vb_ref, df_ref, db_ref, if_ref, ib_ref,
                kk_ref, ka_ref, yf_ref, yb_ref,
                s_ref, a_s, b_s, k_s, rp_s, v_s, y_s, w_s, gp_s, gi_s, g_s, nat_s, *, n_chain):
    @pl.when(pl.program_id(0) == 0)
    def _():
        s_ref[...] = jnp.zeros_like(s_ref)

    lane = lax.broadcasted_iota(jnp.int32, (A_HEAD_DIM, LANES), 1)
    is_fwd = lane < n_chain
    n_batch = n_chain // CHAIN_PAD

    srcs =((rf_ref, rb_ref), (kf_ref, kb_ref), (vf_ref, vb_ref), (if_ref, ib_ref), (df_ref, db_ref))
    q_dec = len(srcs) - 1

    def to_tiles(q, j):
        f_ref, b_ref = srcs[q]
        x = jnp.concatenate([f_ref[b, j] for b in range(n_batch)] + [b_ref[b, j] for b in range(n_batch)],
                            axis=0)
        nat_s[q, 2 * j:2 * j + 2] = x.T.reshape(2, A_HEAD_DIM, LANES)

    def step_tile(q, i):
        return jnp.where(is_fwd, nat_s[q, i], nat_s[q, WKV_TB - 1 - i])

    for j in range(WKV_ROWS):
        to_tiles(q_dec, j)
    g = None
    for i in range(WKV_TB):
        w = jnp.exp(-LOG_DECAY_SCALE * jax.nn.sigmoid(step_tile(q_dec, i)))
        g = w if g is None else g * w
        w_s[i] = w
        gi_s[i] = 1.0 / g
        if i + 1 < WKV_TB:
            gp_s[i + 1] = g
    g_s[...] = g

    kk_gain = kk_ref[...]
    ka = ka_ref[...]
    for ja in range(WKV_ROWS // 2):
        jb = WKV_ROWS - 1 - ja
        for q in range(q_dec):
            to_tiles(q, ja)
            to_tiles(q, jb)
        for i in (2 * ja, 2 * ja + 1, 2 * jb, 2 * jb + 1):
            r, k, v, icl = (step_tile(q, i) for q in range(q_dec))
            icl = jax.nn.sigmoid(icl)
            g_inv = gi_s[i]
            kk = k * kk_gain
            kk = kk * lax.rsqrt(jnp.sum(kk * kk, axis=0, keepdims=True) + 1e-12)
            b = kk * icl
            rp = w_s[i] * r - kk * jnp.sum(b * r, axis=0, keepdims=True)
            v_s[i] = v
            a_s[i] = -kk if i == 0 else -kk * gp_s[i]
            rp_s[i] = rp if i == 0 else rp * gp_s[i]
            b_s[i] = b * g_inv
            k_s[i] = k * (1.0 + (icl - 1.0) * ka) * g_inv

    def step(t, carry):
        rows = A_HEAD_DIM // WKV_VSPLIT
        for part in range(WKV_VSPLIT):
            vs = slice(part * rows, (part + 1) * rows)
            sa_p = [None] * WKV_ACC
            y_p = [None] * WKV_ACC
            for kx in range(A_HEAD_DIM):
                s = s_ref[kx, vs, :]
                pa = s * a_s[t, kx:kx + 1, :]
                py = s * rp_s[t, kx:kx + 1, :]
                i = kx % WKV_ACC
                sa_p[i] = pa if sa_p[i] is None else sa_p[i] + pa
                y_p[i] = py if y_p[i] is None else y_p[i] + py
            sa = functools.reduce(lambda u, w: u + w, sa_p)
            y_s[t, vs, :] = functools.reduce(lambda u, w: u + w, y_p)
            vt = v_s[t, vs, :]
            for kx in range(A_HEAD_DIM):
                s_ref[kx, vs, :] = s_ref[kx, vs, :] + sa * b_s[t, kx:kx + 1, :] + vt * k_s[t, kx:kx + 1, :]
        return carry

    lax.fori_loop(0, WKV_TB, step, 0)

    for kx in range(A_HEAD_DIM):
        s_ref[kx] = s_ref[kx] * g_s[kx:kx + 1, :]

    for j in range(WKV_ROWS):
        y = jnp.concatenate([jnp.where(is_fwd, y_s[i], y_s[WKV_TB - 1 - i]) for i in (2 * j, 2 * j + 1)], axis=0)
        yt = y.T
        for b in range(n_batch):
            yf_ref[b, j] = yt[b * CHAIN_PAD:(b + 1) * CHAIN_PAD]
            yb_ref[b, j] = yt[n_chain + b * CHAIN_PAD:n_chain + (b + 1) * CHAIN_PAD]


def wkv_scan(r, k, v, dec_f, dec_b, icl_f, icl_b, k_k, k_a, n_ctx):
    bsz, t_half, _, _ = r.shape
    n_chain = bsz * CHAIN_PAD
    t_all, kd = 2 * t_half, A_HEAD_DIM
    assert n_ctx % WKV_TB == 0 and t_all % WKV_TB == 0 and 2 * n_chain == LANES
    nb_ctx = n_ctx // WKV_TB
    nb = t_all // WKV_TB

    def bwd_block(j):
        return jnp.where(j < nb_ctx, nb_ctx - 1 - j, nb + nb_ctx - 1 - j)

    f_spec = pl.BlockSpec((bsz, WKV_ROWS, CHAIN_PAD, LANES), lambda j: (0, j, 0, 0))
    b_spec = pl.BlockSpec((bsz, WKV_ROWS, CHAIN_PAD, LANES), lambda j: (0, bwd_block(j), 0, 0))
    tile_spec = pl.BlockSpec((kd, LANES), lambda j: (0, 0))

    def lane_tile(p):
        t = jnp.pad(p.reshape(A_HEADS, kd).T, ((0, 0), (0, CHAIN_PAD - A_HEADS)))
        return jnp.tile(t, (1, LANES // CHAIN_PAD))

    seq = pltpu.VMEM((WKV_TB, kd, LANES), F32)
    out = jax.ShapeDtypeStruct(r.shape, F32)
    yf, yb = pl.pallas_call(
        functools.partial(_wkv_kernel, n_chain=n_chain),
        out_shape=[out, out],
        grid=(nb,),
        in_specs=[f_spec, b_spec] * 5 + [tile_spec, tile_spec],
        out_specs=[f_spec, b_spec],
        scratch_shapes=[pltpu.VMEM((kd, kd, LANES), F32)] + [seq] * 9 + [
            pltpu.VMEM((kd, LANES), F32), pltpu.VMEM((5, WKV_TB, kd, LANES), F32)],
        compiler_params=_params(1),
        name="wkv_scan",
    )(r, r, k, k, v, v, dec_f, dec_b, icl_f, icl_b, lane_tile(k_k), lane_tile(k_a))
    return yf, yb


def _rwkv_post_kernel(yf_ref, yb_ref, vk_ref, bonus_ref, gate_ref, g_ref, b_ref, seg_ref, o_ref, y_s):
    half = RW_TT // 2
    yh = jnp.swapaxes(yf_ref[0] + yb_ref[0], 0, 1)
    for cb in range(A_WIDTH // LANES):
        y0 = yh[2 * cb]
        y1 = yh[2 * cb + 1]
        y_s[cb, pl.ds(0, half, stride=2), :] = jnp.concatenate([y0[:, :A_HEAD_DIM], y1[:, :A_HEAD_DIM]], axis=-1)
        y_s[cb, pl.ds(1, half, stride=2), :] = jnp.concatenate([y0[:, A_HEAD_DIM:], y1[:, A_HEAD_DIM:]], axis=-1)
    y = jnp.concatenate([y_s[cb] for cb in range(A_WIDTH // LANES)], axis=-1) + vk_ref[0]
    seg = seg_ref[...]
    mu = _segsum(y, seg) * (1.0 / A_HEAD_DIM)
    d = y - mu
    var = _segsum(d * d, seg) * (1.0 / A_HEAD_DIM)
    yn = d * lax.rsqrt(var + GN_EPS) * g_ref[...] + b_ref[...]
    o_ref[0] = ((yn + bonus_ref[0]) * gate_ref[0]).astype(o_ref.dtype)


def rwkv_post(yf, yb, vk, bonus, gate, lnx_g, lnx_b, out_dtype=BF16):
    bsz, t_all, _ = vk.shape
    cm_spec = pl.BlockSpec((1, RW_TT // 2, CHAIN_PAD, LANES), lambda b, j: (b, j, 0, 0))
    tm_spec = pl.BlockSpec((1, RW_TT, A_WIDTH), lambda b, j: (b, j, 0))
    return pl.pallas_call(
        _rwkv_post_kernel,
        out_shape=jax.ShapeDtypeStruct((bsz, t_all, A_WIDTH), out_dtype),
        grid=(bsz, t_all // RW_TT),
        in_specs=[cm_spec, cm_spec, tm_spec, tm_spec, tm_spec, _const_spec((1, A_WIDTH), 2),
                  _const_spec((1, A_WIDTH), 2), _const_spec((A_WIDTH, A_WIDTH), 2)],
        out_specs=tm_spec,
        scratch_shapes=[pltpu.VMEM((A_WIDTH // LANES, RW_TT, LANES), F32)],
        compiler_params=_params(2),
        name="rwkv_post",
    )(yf, yb, vk, bonus, gate, lnx_g[None, :], lnx_b[None, :], _seg_matrix())


def _rope_lanes(z, cos, sin_lo, sin_hi):
    return (z * cos + pltpu.roll(z, LANES - SUBLANES, axis=1) * sin_lo + pltpu.roll(z, SUBLANES, axis=1) * sin_hi)


def _mla_q_kernel(pq_ref, gq_ref, w_ref, m_ref, gain_ref, cos_ref, slo_ref, shi_ref, q_o):
    h = _rms_rows(pq_ref[0], gq_ref[...]).astype(BF16)
    q = jnp.dot(h, w_ref[...], preferred_element_type=F32)
    ms = jnp.dot((q * q).astype(BF16), m_ref[...], preferred_element_type=F32)
    qh = q * lax.rsqrt(ms + EPS) * gain_ref[...]
    cos, slo, shi = cos_ref[...], slo_ref[...], shi_ref[...]
    tiles = [_rope_lanes(qh[:, i * HEAD_PAD:(i + 1) * HEAD_PAD], cos, slo, shi) for i in range(B_HEADS)]
    q_o[0] = (jnp.concatenate(tiles, axis=-1) * (ATTN_SCALE * LOG2_E)).astype(BF16)


def _mla_kv_kernel(pkv_ref, gkv_ref, wk_ref, wv_ref, m_ref, gain_ref, gr_ref, cos_ref, slo_ref, shi_ref,
                   place_ref, k_o, v_o):
    x = pkv_ref[0]
    h = _rms_rows(x[:, :KV_LORA], gkv_ref[...]).astype(BF16)
    kn = jnp.dot(h, wk_ref[...], preferred_element_type=F32)
    v = jnp.dot(h, wv_ref[...], preferred_element_type=F32)
    lane = lax.broadcasted_iota(jnp.int32, (v.shape[0], LANES), 1)
    tiles = []
    for pair in range(B_HEADS // 2):
        vp = v[:, pair * LANES:(pair + 1) * LANES]
        tiles += [jnp.where(lane < V_HEAD, vp, 1.0), jnp.where(lane < V_HEAD, 1.0, vp)]
    v_o[0] = jnp.concatenate(tiles, axis=-1).astype(BF16)
    ms = jnp.dot((kn * kn).astype(BF16), m_ref[...], preferred_element_type=F32)
    knh = kn * lax.rsqrt(ms + EPS) * gain_ref[...]
    kr = x[:, KV_LORA:]
    kr = kr * lax.rsqrt(jnp.sum(kr * kr, axis=-1, keepdims=True) * (1.0 / QK_ROPE) + EPS) * gr_ref[...]
    kr = _rope_lanes(kr, cos_ref[...], slo_ref[...], shi_ref[...]).astype(BF16)
    k_o[0] = (knh + jnp.dot(kr, place_ref[...], preferred_element_type=F32)).astype(BF16)


def _head_pattern(nope, rope):
    tile = jnp.concatenate([nope, rope, jnp.zeros((HEAD_PAD - QK_NOPE - QK_ROPE,), F32)])
    return jnp.tile(tile, B_HEADS)[None, :]


def _seg_mean_matrix(with_rope):
    i = jnp.arange(QP_WIDTH)
    head, off = i // HEAD_PAD, i % HEAD_PAD
    nope = off < QK_NOPE
    rope = jnp.logical_and(off >= QK_NOPE, off < QK_NOPE + QK_ROPE)
    same = head[:, None] == head[None, :]
    m = jnp.where(same & nope[:, None] & nope[None, :], 1.0 / QK_NOPE, 0.0)
    if with_rope:
        m = m + jnp.where(same & rope[:, None] & rope[None, :], 1.0 / QK_ROPE, 0.0)
    return m.astype(BF16)


def _rope_tables(n, n_ctx):
    t = jnp.arange(n, dtype=jnp.int32)
    r_pos = (t // GRID_W).astype(F32)
    c_pos = (t % GRID_W).astype(F32)
    axis_dim = QK_ROPE // 2
    inv_freq = 1.0 / (ROPE_BASE ** (jnp.arange(0, axis_dim, 2, dtype=F32) / axis_dim))
    ang_r = r_pos[:, None] * inv_freq
    ang_c = c_pos[:, None] * inv_freq
    half = axis_dim // 2
    z = jnp.zeros((n, half), F32)
    cos = jnp.concatenate([jnp.cos(ang_r)] * 2 + [jnp.cos(ang_c)] * 2, axis=1)
    s_lo = jnp.concatenate([-jnp.sin(ang_r), z, -jnp.sin(ang_c), z], axis=1)
    s_hi = jnp.concatenate([z, jnp.sin(ang_r), z, jnp.sin(ang_c)], axis=1)

    def widen(tab, fill):
        tab = jnp.concatenate([jnp.full((n_ctx, QK_ROPE), fill, F32), tab], axis=0)
        return jnp.concatenate([jnp.full((n_ctx + n, QK_NOPE), fill, F32), tab,
                                jnp.full((n_ctx + n, HEAD_PAD - QK_NOPE - QK_ROPE), fill, F32)], axis=1)

    return widen(cos, 1.0), widen(s_lo, 0.0), widen(s_hi, 0.0)


def mla_qkv(p_q, p_kv, tabs, q_norm_g, kv_norm_g, w_uq, w_ukv, q_nope_g, k_nope_g, q_rope_g, k_rope_g):
    bsz, t_all, _ = p_q.shape
    nblk = t_all // BM
    dq = QK_NOPE + QK_ROPE
    w_q = jnp.pad(w_uq.reshape(Q_LORA, B_HEADS, dq), ((0, 0), (0, 0), (0, HEAD_PAD - dq))).reshape(
        Q_LORA, QP_WIDTH).astype(BF16)
    w4 = w_ukv.reshape(KV_LORA, B_HEADS, QK_NOPE + V_HEAD)
    w_k = jnp.pad(w4[:, :, :QK_NOPE], ((0, 0), (0, 0), (0, HEAD_PAD - QK_NOPE))).reshape(
        KV_LORA, QP_WIDTH).astype(BF16)
    w_v = w4[:, :, QK_NOPE:].reshape(KV_LORA, B_WIDTH).astype(BF16)
    zero_r = jnp.zeros((QK_ROPE,), F32)
    i = jnp.arange(LANES)
    rope_lane = jnp.logical_and(i >= QK_NOPE, i < QK_NOPE + QK_ROPE)
    place = jnp.where(rope_lane[:, None] & (i[:, None] == (jnp.arange(QP_WIDTH) % HEAD_PAD)[None, :]),
                      1.0, 0.0).astype(BF16)
    gr = jnp.concatenate([jnp.zeros((QK_NOPE,), F32), k_rope_g, jnp.zeros((LANES - QK_NOPE - QK_ROPE,), F32)])

    row_spec = lambda n: pl.BlockSpec((1, BM, n), lambda j, b: (b, j, 0))
    tab_spec = pl.BlockSpec((BM, LANES), lambda j, b: (j, 0))
    c = lambda shape: _const_spec(shape, 2)
    q = pl.pallas_call(
        _mla_q_kernel,
        out_shape=jax.ShapeDtypeStruct((bsz, t_all, QP_WIDTH), BF16),
        grid=(nblk, bsz),
        in_specs=[row_spec(Q_LORA), c((1, Q_LORA)), c((Q_LORA, QP_WIDTH)), c((QP_WIDTH, QP_WIDTH)),
                  c((1, QP_WIDTH)), tab_spec, tab_spec, tab_spec],
        out_specs=row_spec(QP_WIDTH),
        compiler_params=_params(2),
        name="mla_q",
    )(p_q, q_norm_g[None, :], w_q, _seg_mean_matrix(True), _head_pattern(q_nope_g, q_rope_g), *tabs)
    k, v = pl.pallas_call(
        _mla_kv_kernel,
        out_shape=[jax.ShapeDtypeStruct((bsz, t_all, QP_WIDTH), BF16),
                   jax.ShapeDtypeStruct((bsz, t_all, QP_WIDTH), BF16)],
        grid=(nblk, bsz),
        in_specs=[row_spec(KV_IN_PAD), c((1, KV_LORA)), c((KV_LORA, QP_WIDTH)), c((KV_LORA, B_WIDTH)),
                  c((QP_WIDTH, QP_WIDTH)), c((1, QP_WIDTH)), c((1, LANES)), tab_spec, tab_spec, tab_spec,
                  c((LANES, QP_WIDTH))],
        out_specs=[row_spec(QP_WIDTH), row_spec(QP_WIDTH)],
        compiler_params=_params(2),
        name="mla_kv",
    )(p_kv, kv_norm_g[None, :], w_k, w_v, _seg_mean_matrix(False), _head_pattern(k_nope_g, zero_r),
      gr[None, :], *tabs, place)
    return q, k, v


ATT_BQ = 256
HEADS_PER_STEP = 2


def _attn_kernel(q_ref, k_ref, v_ref, o_ref):
    scores = []
    for h in range(HEADS_PER_STEP):
        q = q_ref[0, :, h * HEAD_PAD:(h + 1) * HEAD_PAD]
        k = k_ref[0, :, h * HEAD_PAD:(h + 1) * HEAD_PAD]
        scores.append(lax.dot_general(q, k, (((1,), (1,)), ((), ())), preferred_element_type=F32))
    outs = []
    for h in range(HEADS_PER_STEP):
        s = scores[h]
        p = jnp.exp2(s - jnp.max(s, axis=-1, keepdims=True)).astype(BF16)
        outs.append(jnp.dot(p, v_ref[0, :, h * HEAD_PAD:(h + 1) * HEAD_PAD], preferred_element_type=F32))
    lane = lax.broadcasted_iota(jnp.int32, outs[0].shape, 1)
    first = lane < V_HEAD
    num = jnp.where(first, outs[0], outs[1])
    den = pltpu.roll(jnp.where(first, outs[1], outs[0]), V_HEAD, axis=1)
    o_ref[0] = (num / den).astype(o_ref.dtype)


def attention(q, k, v, q_row0, nq, nk):
    bsz = q.shape[0]
    bq = min(ATT_BQ, nq)
    qb0 = q_row0 // bq
    pair = HEADS_PER_STEP * HEAD_PAD
    return pl.pallas_call(
        _attn_kernel,
        out_shape=jax.ShapeDtypeStruct((bsz, nq, B_WIDTH), BF16),
        grid=(bsz, B_HEADS // HEADS_PER_STEP, nq // bq),
        in_specs=[pl.BlockSpec((1, bq, pair), lambda b, hp, i: (b, qb0 + i, hp)),
                  pl.BlockSpec((1, nk, pair), lambda b, hp, i: (b, 0, hp)),
                  pl.BlockSpec((1, nk, pair), lambda b, hp, i: (b, 0, hp))],
        out_specs=pl.BlockSpec((1, bq, LANES), lambda b, hp, i: (b, i, hp)),
        compiler_params=_params(3),
        name="mla_attention",
    )(q, k, v)


def _outproj_kernel(ya_ref, ybc_ref, ybl_ref, pc_ref, pcp_ref, pcn_ref, x_ref, mod_ref, cw_ref, w_ref, o_ref, *,
                    row_off, n_ctx, t_all):
    j = pl.program_id(1) + row_off
    t0 = j * BM
    first = jnp.logical_or(t0 == 0, t0 == n_ctx)
    last = jnp.logical_or(t0 + BM == n_ctx, t0 + BM == t_all)
    o1, o2 = C_WIDTH, 2 * C_WIDTH
    pc = pc_ref[0]
    u = pc[:, o1:o2] * pc[:, o2:]
    pcp = pcp_ref[0, SUBLANES - 1:SUBLANES, :]
    pcn = pcn_ref[0, 0:1, :]
    u_prev = jnp.where(first, 0.0, pcp[:, o1:o2] * pcp[:, o2:])
    u_next = jnp.where(last, 0.0, pcn[:, o1:o2] * pcn[:, o2:])
    rows = lax.broadcasted_iota(jnp.int32, u.shape, 0)
    up = jnp.where(rows == 0, u_prev, pltpu.roll(u, 1, axis=0))
    un = jnp.where(rows == BM - 1, u_next, pltpu.roll(u, BM - 1, axis=0))
    yc = pc[:, :o1] * (cw_ref[0:1, :] * up + cw_ref[1:2, :] * u + cw_ref[2:3, :] * un)
    yb = jnp.where(j == 0, ybc_ref[0], ybl_ref[0]) if row_off == 0 else ybl_ref[0]
    acc = jnp.dot(ya_ref[0], w_ref[0:A_WIDTH, :], preferred_element_type=F32)
    acc += jnp.dot(yb, w_ref[A_WIDTH:A_WIDTH + B_WIDTH, :], preferred_element_type=F32)
    acc += jnp.dot(yc.astype(BF16), w_ref[A_WIDTH + B_WIDTH:, :], preferred_element_type=F32)
    o_ref[0] = x_ref[0] + mod_ref[0, :, 2 * D_MODEL:3 * D_MODEL] * acc


def outproj(ya, yb_ctx, yb_lat, p_c, xa, mod, conv_w, w_out, n_ctx, with_ctx):
    bsz, t_all, d = xa.shape
    row_off = 0 if with_ctx else n_ctx // BM
    nblk = t_all // BM - row_off
    nb8 = t_all // SUBLANES
    per8 = BM // SUBLANES
    rows = lambda n: pl.BlockSpec((1, BM, n), lambda b, j: (b, j + row_off, 0))
    lat_off = n_ctx // BM
    return pl.pallas_call(
        functools.partial(_outproj_kernel, row_off=row_off, n_ctx=n_ctx, t_all=t_all),
        out_shape=jax.ShapeDtypeStruct((bsz, nblk * BM, d), F32),
        grid=(bsz, nblk),
        in_specs=[rows(A_WIDTH),
                  pl.BlockSpec((1, BM, B_WIDTH), lambda b, j: (b, 0, 0)),
                  pl.BlockSpec((1, BM, B_WIDTH), lambda b, j: (b, jnp.maximum(j + row_off - lat_off, 0), 0)),
                  rows(C_IN),
                  pl.BlockSpec((1, SUBLANES, C_IN), lambda b, j: (b, jnp.maximum((j + row_off) * per8 - 1, 0), 0)),
                  pl.BlockSpec((1, SUBLANES, C_IN),
                               lambda b, j: (b, jnp.minimum((j + row_off + 1) * per8, nb8 - 1), 0)),
                  rows(d),
                  pl.BlockSpec((1, 1, 6 * d), lambda b, j: (2 * b + jnp.minimum(j + row_off, 1), 0, 0)),
                  _const_spec((3, C_WIDTH), 2), _const_spec((d, d), 2)],
        out_specs=pl.BlockSpec((1, BM, d), lambda b, j: (b, j, 0)),
        compiler_params=_params(2),
        name="outproj",
    )(ya, yb_ctx, yb_lat, p_c, p_c, p_c, xa, mod, conv_w, w_out)


FFN_CHUNKS = 2


def _ffn_in_kernel(x_ref, mod_ref, g_ref, w_ref, o_ref):
    sh = mod_ref[0, :, 3 * D_MODEL:4 * D_MODEL]
    sc = mod_ref[0, :, 4 * D_MODEL:5 * D_MODEL]
    h = (_rms_rows(x_ref[0], g_ref[...]) * (1.0 + sc) + sh).astype(BF16)
    d_ff = o_ref.shape[-1]
    cw = d_ff // FFN_CHUNKS
    for c in range(FFN_CHUNKS):
        gate = jnp.dot(h, w_ref[:, c * cw:(c + 1) * cw], preferred_element_type=F32)
        up = jnp.dot(h, w_ref[:, d_ff + c * cw:d_ff + (c + 1) * cw], preferred_element_type=F32)
        o_ref[0, :, c * cw:(c + 1) * cw] = (gate * jax.nn.sigmoid(gate) * up).astype(o_ref.dtype)


def _ffn_out_kernel(a_ref, x_ref, mod_ref, w_ref, o_ref):
    acc = jnp.dot(a_ref[0], w_ref[...], preferred_element_type=F32)
    o_ref[0] = x_ref[0] + mod_ref[0, :, 5 * D_MODEL:6 * D_MODEL] * acc


def ffn(x1, mod, g, w_fi, w_fo, lat_only):
    bsz, t, d = x1.shape
    d_ff = w_fo.shape[0]
    seg = 1 if lat_only else 0
    rows = lambda n: pl.BlockSpec((1, BM, n), lambda b, j: (b, j, 0))
    mod_spec = pl.BlockSpec((1, 1, 6 * d), lambda b, j: (2 * b + jnp.minimum(j + seg, 1), 0, 0))
    act = pl.pallas_call(
        _ffn_in_kernel,
        out_shape=jax.ShapeDtypeStruct((bsz, t, d_ff), BF16),
        grid=(bsz, t // BM),
        in_specs=[rows(d), mod_spec, _const_spec((1, d), 2), _const_spec((d, 2 * d_ff), 2)],
        out_specs=rows(d_ff),
        compiler_params=_params(2),
        name="ffn_in",
    )(x1, mod, g[None, :], w_fi)
    return pl.pallas_call(
        _ffn_out_kernel,
        out_shape=jax.ShapeDtypeStruct((bsz, t, d), F32),
        grid=(bsz, t // BM),
        in_specs=[rows(d_ff), rows(d), mod_spec, _const_spec((d_ff, d), 2)],
        out_specs=rows(d),
        compiler_params=_params(2),
        name="ffn_out",
    )(act, x1, mod, w_fo)


def kernel(x, c, ctx, c_ctx, ada_w, ada_b, norm1_g, norm2_g, w_in, tshift_mu, decay_w0, decay_up, icl_a0, icl_up, gate_up, k_k, k_a, r_k, lnx_g, lnx_b, q_norm_g, kv_norm_g, w_uq, w_ukv, q_nope_g, k_nope_g, q_rope_g, k_rope_g, conv_w, w_out, w_ffn_in, w_ffn_out):
    bsz, n, d = x.shape
    n_ctx = ctx.shape[1]
    depth = ada_w.shape[0]
    assert n_ctx == BM and n % BM == 0 and d == D_MODEL
    tabs = _rope_tables(n, n_ctx)

    xa = jnp.concatenate([ctx, x], axis=1)
    cond = jnp.concatenate([c, c_ctx[None, :], jnp.zeros((2 * SUBLANES - bsz - 1, d), c.dtype)], axis=0)
    silu_cond = cond * jax.nn.sigmoid(cond)
    o_kv = A_IN + Q_LORA + KV_LORA
    for l in range(depth):
        last = l == depth - 1
        ada = pmm(silu_cond, ada_w, l) + ada_b[l]
        mod = jnp.stack([jnp.broadcast_to(ada[bsz], (bsz, 6 * d)), ada[:bsz]], axis=1).reshape(2 * bsz, 1, 6 * d)

        wl = w_in[l]
        w_p = jnp.concatenate(
            [wl[:, :o_kv], jnp.zeros((d, QK_NOPE), F32), wl[:, o_kv:A_IN + B_IN],
             jnp.zeros((d, LANES - QK_NOPE - QK_ROPE), F32), wl[:, A_IN + B_IN:]], axis=1).astype(BF16)
        p_q, p_kv, p_c, r_s, k_s, v_s, dec_f, dec_b, icl_f, icl_b, vk, bonus, gate = inproj(
            xa, mod, norm1_g[l], w_p, n_ctx, tshift_mu[l], decay_w0[l], decay_up[l], icl_a0[l], icl_up[l],
            gate_up[l], k_a[l], r_k[l])
        yf, yr = wkv_scan(r_s, k_s, v_s, dec_f, dec_b, icl_f, icl_b, k_k[l], k_a[l], n_ctx)
        ya = rwkv_post(yf, yr, vk, bonus, gate, lnx_g[l], lnx_b[l])
        q, k, v = mla_qkv(p_q, p_kv, tabs, q_norm_g[l], kv_norm_g[l], w_uq[l], w_ukv[l], q_nope_g[l],
                          k_nope_g[l], q_rope_g[l], k_rope_g[l])
        yb_lat = attention(q, k, v, n_ctx, n, n_ctx + n)
        yb_ctx = yb_lat if last else attention(q, k, v, 0, n_ctx, n_ctx)

        x1 = outproj(ya, yb_ctx, yb_lat, p_c, xa, mod, conv_w[l], w_out[l].astype(BF16), n_ctx, not last)
        xa = ffn(x1, mod, norm2_g[l], w_ffn_in[l].astype(BF16), w_ffn_out[l].astype(BF16), last)
    return xa
```

```python
import functools

import jax
import jax.numpy as jnp
from jax import lax
from jax.experimental import pallas as pl
from jax.experimental.pallas import tpu as pltpu

D_MODEL = 1024
GRID_W = 64
A_HEADS = 6
A_HEAD_DIM = 64
A_WIDTH = A_HEADS * A_HEAD_DIM
DECAY_LORA = 64
ICL_LORA = 64
GATE_LORA = 128
LOG_DECAY_SCALE = 0.606531
GN_EPS = A_HEAD_DIM * 1e-5
B_HEADS = 6
QK_NOPE = 64
QK_ROPE = 32
V_HEAD = 64
B_WIDTH = B_HEADS * V_HEAD
Q_LORA = 768
KV_LORA = 256
ROPE_BASE = 10000.0
ATTN_SCALE = (QK_NOPE + QK_ROPE) ** -0.5
LOG2_E = 1.4426950408889634
C_WIDTH = D_MODEL - A_WIDTH - B_WIDTH
A_IN = 3 * A_WIDTH + DECAY_LORA + ICL_LORA + GATE_LORA
B_IN = Q_LORA + KV_LORA + QK_ROPE
C_IN = 3 * C_WIDTH
EPS = 1e-6

LANES = 128
SUBLANES = 8
HEAD_PAD = LANES
QP_WIDTH = B_HEADS * HEAD_PAD
KV_IN_PAD = KV_LORA + LANES
P_PAD = A_IN + Q_LORA + KV_IN_PAD + C_IN
VMEM_LIMIT = 48 * 1024 * 1024
BM = 256
F32 = jnp.float32
BF16 = jnp.bfloat16


def _const_spec(shape, nargs):
    zeros = tuple(0 for _ in shape)
    if nargs == 1:
        return pl.BlockSpec(shape, lambda i: zeros)
    if nargs == 2:
        return pl.BlockSpec(shape, lambda i, j: zeros)
    return pl.BlockSpec(shape, lambda i, j, k: zeros)


def _params(n_axes):
    return pltpu.CompilerParams(dimension_semantics=("arbitrary",) * n_axes, vmem_limit_bytes=VMEM_LIMIT)


def _rms_rows(x, g):
    return x * lax.rsqrt(jnp.mean(x * x, axis=-1, keepdims=True) + EPS) * g


def _mm_kernel(x_ref, w_ref, o_ref):
    o_ref[...] = jnp.dot(x_ref[...].astype(BF16), w_ref[0].astype(BF16), preferred_element_type=F32)


def pmm(x, w, layer, bn=1024):
    m, k = x.shape
    n = w.shape[2]
    return pl.pallas_call(
        _mm_kernel,
        out_shape=jax.ShapeDtypeStruct((m, n), F32),
        grid=(n // bn,),
        in_specs=[pl.BlockSpec((m, k), lambda j: (0, 0)), pl.BlockSpec((1, k, bn), lambda j: (layer, 0, j))],
        out_specs=pl.BlockSpec((m, bn), lambda j: (0, j)),
        compiler_params=_params(1),
        name="pmm",
    )(x, w)


def _seg_matrix():
    h = jnp.arange(A_WIDTH) // A_HEAD_DIM
    return (h[:, None] == h[None, :]).astype(BF16)


def _segsum(z, seg):
    hi = z.astype(BF16)
    lo = (z - hi.astype(F32)).astype(BF16)
    return jnp.dot(hi, seg, preferred_element_type=F32) + jnp.dot(lo, seg, preferred_element_type=F32)


RW_TT = 128
CHAIN_PAD = SUBLANES


def _rwkv_prep(x, prev_row, next_row, mu_ref, wlo_ref, wg_ref, bias_ref, ka_ref, rk_ref, seg_ref,
               r_o, k_o, v_o, df_o, db_o, if_o, ib_o, vk_o, bonus_o, gate_o, pk_s):
    tt = x.shape[0]
    rows = lax.broadcasted_iota(jnp.int32, x.shape, 0)
    xp = jnp.where(rows == 0, prev_row, pltpu.roll(x, 1, axis=0))
    xn = jnp.where(rows == tt - 1, next_row, pltpu.roll(x, tt - 1, axis=0))
    xs = x + mu_ref[0:1, :] * (xp - x) + mu_ref[1:2, :] * (xn - x)

    o1, o2, o3 = A_WIDTH, 2 * A_WIDTH, 3 * A_WIDTH
    o5 = o3 + DECAY_LORA + ICL_LORA
    r, k, v = xs[:, :o1], xs[:, o1:o2], xs[:, o2:o3]
    lo2 = xs[:, o3:o5]
    lane = lax.broadcasted_iota(jnp.int32, lo2.shape, 1)
    lo2 = jnp.where(lane < DECAY_LORA, jnp.tanh(lo2), lo2).astype(BF16)
    lo_out = jnp.dot(lo2, wlo_ref[...], preferred_element_type=F32) + bias_ref[...]
    gate = jnp.dot(jax.nn.sigmoid(xs[:, o5:]).astype(BF16), wg_ref[...], preferred_element_type=F32)
    dec_f, dec_b = lo_out[:, :o1], lo_out[:, o1:o2]
    icl_f, icl_b = lo_out[:, o2:o3], lo_out[:, o3:]
    m = 2.0 + (jax.nn.sigmoid(icl_f) + jax.nn.sigmoid(icl_b) - 2.0) * ka_ref[...]
    rkm = r * k * m
    seg = seg_ref[...]
    vk_o[0] = v * _segsum(rkm, seg)
    bonus_o[0] = v * _segsum(rkm * rk_ref[...], seg)
    gate_o[0] = gate
    half = tt // 2
    zero = jnp.zeros((half, LANES), F32)
    for i, (val, out) in enumerate(((r, r_o), (k, k_o), (v, v_o), (dec_f, df_o), (dec_b, db_o),
                                    (icl_f, if_o), (icl_b, ib_o))):
        heads = []
        for cb in range(A_WIDTH // LANES):
            pk_s[i, cb] = val[:, cb * LANES:(cb + 1) * LANES]
            even = pk_s[i, cb, pl.ds(0, half, stride=2), :]
            odd = pk_s[i, cb, pl.ds(1, half, stride=2), :]
            for hh in range(2):
                sl = slice(hh * A_HEAD_DIM, (hh + 1) * A_HEAD_DIM)
                heads.append(jnp.concatenate([even[:, sl], odd[:, sl]], axis=-1))
        heads += [zero] * (CHAIN_PAD - A_HEADS)
        out[0] = jnp.swapaxes(jnp.stack(heads, axis=0), 0, 1)


def _inproj_kernel(x_ref, xprev_ref, xnext_ref, mod_ref, g_ref, w_ref, *rest, n_ctx, t_all):
    prep_consts, (pq_o, pkv_o, pc_o), prep_outs = rest[:7], rest[7:10], rest[10:]
    t0 = pl.program_id(1) * BM
    first = jnp.logical_or(t0 == 0, t0 == n_ctx)
    last = jnp.logical_or(t0 + BM == n_ctx, t0 + BM == t_all)
    sh = mod_ref[0, :, 0:D_MODEL]
    sc = mod_ref[0, :, D_MODEL:2 * D_MODEL]
    x_ext = jnp.concatenate([xprev_ref[0], x_ref[0], xnext_ref[0]], axis=0)
    h_ext = (_rms_rows(x_ext, g_ref[...]) * (1.0 + sc) + sh).astype(BF16)
    pa = jnp.dot(h_ext, w_ref[:, :A_IN], preferred_element_type=F32)
    prev_row = jnp.where(first, 0.0, pa[SUBLANES - 1:SUBLANES])
    next_row = jnp.where(last, 0.0, pa[SUBLANES + BM:SUBLANES + BM + 1])
    _rwkv_prep(pa[SUBLANES:SUBLANES + BM], prev_row, next_row, *prep_consts, *prep_outs)
    h = h_ext[SUBLANES:SUBLANES + BM]
    o = A_IN
    for out in (pq_o, pkv_o, pc_o):
        n = out.shape[-1]
        out[0] = jnp.dot(h, w_ref[:, o:o + n], preferred_element_type=F32)
        o += n


def inproj(xa, mod, g, w_p, n_ctx, tshift_mu, decay_w0, decay_up, icl_a0, icl_up, gate_up, k_a, r_k):
    bsz, t_all, d = xa.shape
    nblk = t_all // BM
    zeros = jnp.zeros((DECAY_LORA, 2 * A_WIDTH), F32)
    wlo = jnp.concatenate([
        jnp.concatenate([decay_up[0], decay_up[1], zeros], axis=1),
        jnp.concatenate([zeros, icl_up[0], icl_up[1]], axis=1)], axis=0).astype(BF16)
    bias = jnp.concatenate([decay_w0[0], decay_w0[1], icl_a0[0], icl_a0[1]])[None, :]
    nb8 = t_all // SUBLANES
    per8 = BM // SUBLANES
    widths = (Q_LORA, KV_IN_PAD, C_IN)
    cm = jax.ShapeDtypeStruct((bsz, t_all // 2, CHAIN_PAD, LANES), F32)
    tm = jax.ShapeDtypeStruct((bsz, t_all, A_WIDTH), F32)
    cm_spec = pl.BlockSpec((1, BM // 2, CHAIN_PAD, LANES), lambda b, j: (b, j, 0, 0))
    rows = lambda n: pl.BlockSpec((1, BM, n), lambda b, j: (b, j, 0))
    return pl.pallas_call(
        functools.partial(_inproj_kernel, n_ctx=n_ctx, t_all=t_all),
        out_shape=[jax.ShapeDtypeStruct((bsz, t_all, n), F32) for n in widths] + [cm] * 7 + [tm] * 3,
        grid=(bsz, nblk),
        in_specs=[rows(d),
                  pl.BlockSpec((1, SUBLANES, d), lambda b, j: (b, jnp.maximum(j * per8 - 1, 0), 0)),
                  pl.BlockSpec((1, SUBLANES, d), lambda b, j: (b, jnp.minimum((j + 1) * per8, nb8 - 1), 0)),
                  pl.BlockSpec((1, 1, 6 * d), lambda b, j: (2 * b + jnp.minimum(j, 1), 0, 0)),
                  _const_spec((1, d), 2), _const_spec((d, P_PAD), 2),
                  _const_spec((2, A_IN), 2), _const_spec((2 * DECAY_LORA, 4 * A_WIDTH), 2),
                  _const_spec((GATE_LORA, A_WIDTH), 2), _const_spec((1, 4 * A_WIDTH), 2),
                  _const_spec((1, A_WIDTH), 2), _const_spec((1, A_WIDTH), 2), _const_spec((A_WIDTH, A_WIDTH), 2)],
        out_specs=[rows(n) for n in widths] + [cm_spec] * 7 + [rows(A_WIDTH)] * 3,
        scratch_shapes=[pltpu.VMEM((7, A_WIDTH // LANES, BM, LANES), F32)],
        compiler_params=_params(2),
        name="inproj",
    )(xa, xa, xa, mod, g[None, :], w_p, tshift_mu, wlo, gate_up.astype(BF16), bias, k_a[None, :],
      r_k.reshape(1, A_WIDTH), _seg_matrix())


WKV_TB = 32
WKV_ROWS = WKV_TB // 2
WKV_ACC = 2
WKV_VSPLIT = 2


def _wkv_kernel(rf_ref, rb_ref, kf_ref, kb_ref, vf_ref, vb_ref, df_ref, db_ref, if_ref, ib_ref,
                kk_ref, ka_ref, yf_ref, yb_ref,
                s_ref, a_s, b_s, k_s, rp_s, v_s, y_s, w_s, gp_s, gi_s, g_s, nat_s, *, n_chain):
    @pl.when(pl.program_id(0) == 0)
    def _():
        s_ref[...] = jnp.zeros_like(s_ref)

    lane = lax.broadcasted_iota(jnp.int32, (A_HEAD_DIM, LANES), 1)
    is_fwd = lane < n_chain
    n_batch = n_chain // CHAIN_PAD

    srcs =((rf_ref, rb_ref), (kf_ref, kb_ref), (vf_ref, vb_ref), (if_ref, ib_ref), (df_ref, db_ref))
    q_dec = len(srcs) - 1

    def to_tiles(q, j):
        f_ref, b_ref = srcs[q]
        x = jnp.concatenate([f_ref[b, j] for b in range(n_batch)] + [b_ref[b, j] for b in range(n_batch)],
                            axis=0)
        nat_s[q, 2 * j:2 * j + 2] = x.T.reshape(2, A_HEAD_DIM, LANES)

    def step_tile(q, i):
        return jnp.where(is_fwd, nat_s[q, i], nat_s[q, WKV_TB - 1 - i])

    for j in range(WKV_ROWS):
        to_tiles(q_dec, j)
    g = None
    for i in range(WKV_TB):
        w = jnp.exp(-LOG_DECAY_SCALE * jax.nn.sigmoid(step_tile(q_dec, i)))
        g = w if g is None else g * w
        w_s[i] = w
        gi_s[i] = 1.0 / g
        if i + 1 < WKV_TB:
            gp_s[i + 1] = g
    g_s[...] = g

    kk_gain = kk_ref[...]
    ka = ka_ref[...]
    for ja in range(WKV_ROWS // 2):
        jb = WKV_ROWS - 1 - ja
        for q in range(q_dec):
            to_tiles(q, ja)
            to_tiles(q, jb)
        for i in (2 * ja, 2 * ja + 1, 2 * jb, 2 * jb + 1):
            r, k, v, icl = (step_tile(q, i) for q in range(q_dec))
            icl = jax.nn.sigmoid(icl)
            g_inv = gi_s[i]
            kk = k * kk_gain
            kk = kk * lax.rsqrt(jnp.sum(kk * kk, axis=0, keepdims=True) + 1e-12)
            b = kk * icl
            rp = w_s[i] * r - kk * jnp.sum(b * r, axis=0, keepdims=True)
            v_s[i] = v
            a_s[i] = -kk if i == 0 else -kk * gp_s[i]
            rp_s[i] = rp if i == 0 else rp * gp_s[i]
            b_s[i] = b * g_inv
            k_s[i] = k * (1.0 + (icl - 1.0) * ka) * g_inv

    def step(t, carry):
        rows = A_HEAD_DIM // WKV_VSPLIT
        for part in range(WKV_VSPLIT):
            vs = slice(part * rows, (part + 1) * rows)
            sa_p = [None] * WKV_ACC
            y_p = [None] * WKV_ACC
            for kx in range(A_HEAD_DIM):
                s = s_ref[kx, vs, :]
                pa = s * a_s[t, kx:kx + 1, :]
                py = s * rp_s[t, kx:kx + 1, :]
                i = kx % WKV_ACC
                sa_p[i] = pa if sa_p[i] is None else sa_p[i] + pa
                y_p[i] = py if y_p[i] is None else y_p[i] + py
            sa = functools.reduce(lambda u, w: u + w, sa_p)
            y_s[t, vs, :] = functools.reduce(lambda u, w: u + w, y_p)
            vt = v_s[t, vs, :]
            for kx in range(A_HEAD_DIM):
                s_ref[kx, vs, :] = s_ref[kx, vs, :] + sa * b_s[t, kx:kx + 1, :] + vt * k_s[t, kx:kx + 1, :]
        return carry

    lax.fori_loop(0, WKV_TB, step, 0)

    for kx in range(A_HEAD_DIM):
        s_ref[kx] = s_ref[kx] * g_s[kx:kx + 1, :]

    for j in range(WKV_ROWS):
        y = jnp.concatenate([jnp.where(is_fwd, y_s[i], y_s[WKV_TB - 1 - i]) for i in (2 * j, 2 * j + 1)], axis=0)
        yt = y.T
        for b in range(n_batch):
            yf_ref[b, j] = yt[b * CHAIN_PAD:(b + 1) * CHAIN_PAD]
            yb_ref[b, j] = yt[n_chain + b * CHAIN_PAD:n_chain + (b + 1) * CHAIN_PAD]


def wkv_scan(r, k, v, dec_f, dec_b, icl_f, icl_b, k_k, k_a, n_ctx):
    bsz, t_half, _, _ = r.shape
    n_chain = bsz * CHAIN_PAD
    t_all, kd = 2 * t_half, A_HEAD_DIM
    assert n_ctx % WKV_TB == 0 and t_all % WKV_TB == 0 and 2 * n_chain == LANES
    nb_ctx = n_ctx // WKV_TB
    nb = t_all // WKV_TB

    def bwd_block(j):
        return jnp.where(j < nb_ctx, nb_ctx - 1 - j, nb + nb_ctx - 1 - j)

    f_spec = pl.BlockSpec((bsz, WKV_ROWS, CHAIN_PAD, LANES), lambda j: (0, j, 0, 0))
    b_spec = pl.BlockSpec((bsz, WKV_ROWS, CHAIN_PAD, LANES), lambda j: (0, bwd_block(j), 0, 0))
    tile_spec = pl.BlockSpec((kd, LANES), lambda j: (0, 0))

    def lane_tile(p):
        t = jnp.pad(p.reshape(A_HEADS, kd).T, ((0, 0), (0, CHAIN_PAD - A_HEADS)))
        return jnp.tile(t, (1, LANES // CHAIN_PAD))

    seq = pltpu.VMEM((WKV_TB, kd, LANES), F32)
    out = jax.ShapeDtypeStruct(r.shape, F32)
    yf, yb = pl.pallas_call(
        functools.partial(_wkv_kernel, n_chain=n_chain),
        out_shape=[out, out],
        grid=(nb,),
        in_specs=[f_spec, b_spec] * 5 + [tile_spec, tile_spec],
        out_specs=[f_spec, b_spec],
        scratch_shapes=[pltpu.VMEM((kd, kd, LANES), F32)] + [seq] * 9 + [
            pltpu.VMEM((kd, LANES), F32), pltpu.VMEM((5, WKV_TB, kd, LANES), F32)],
        compiler_params=_params(1),
        name="wkv_scan",
    )(r, r, k, k, v, v, dec_f, dec_b, icl_f, icl_b, lane_tile(k_k), lane_tile(k_a))
    return yf, yb


def _rwkv_post_kernel(yf_ref, yb_ref, vk_ref, bonus_ref, gate_ref, g_ref, b_ref, seg_ref, o_ref, y_s):
    half = RW_TT // 2
    yh = jnp.swapaxes(yf_ref[0] + yb_ref[0], 0, 1)
    for cb in range(A_WIDTH // LANES):
        y0 = yh[2 * cb]
        y1 = yh[2 * cb + 1]
        y_s[cb, pl.ds(0, half, stride=2), :] = jnp.concatenate([y0[:, :A_HEAD_DIM], y1[:, :A_HEAD_DIM]], axis=-1)
        y_s[cb, pl.ds(1, half, stride=2), :] = jnp.concatenate([y0[:, A_HEAD_DIM:], y1[:, A_HEAD_DIM:]], axis=-1)
    y = jnp.concatenate([y_s[cb] for cb in range(A_WIDTH // LANES)], axis=-1) + vk_ref[0]
    seg = seg_ref[...]
    mu = _segsum(y, seg) * (1.0 / A_HEAD_DIM)
    d = y - mu
    var = _segsum(d * d, seg) * (1.0 / A_HEAD_DIM)
    yn = d * lax.rsqrt(var + GN_EPS) * g_ref[...] + b_ref[...]
    o_ref[0] = ((yn + bonus_ref[0]) * gate_ref[0]).astype(o_ref.dtype)


def rwkv_post(yf, yb, vk, bonus, gate, lnx_g, lnx_b, out_dtype=BF16):
    bsz, t_all, _ = vk.shape
    cm_spec = pl.BlockSpec((1, RW_TT // 2, CHAIN_PAD, LANES), lambda b, j: (b, j, 0, 0))
    tm_spec = pl.BlockSpec((1, RW_TT, A_WIDTH), lambda b, j: (b, j, 0))
    return pl.pallas_call(
        _rwkv_post_kernel,
        out_shape=jax.ShapeDtypeStruct((bsz, t_all, A_WIDTH), out_dtype),
        grid=(bsz, t_all // RW_TT),
        in_specs=[cm_spec, cm_spec, tm_spec, tm_spec, tm_spec, _const_spec((1, A_WIDTH), 2),
                  _const_spec((1, A_WIDTH), 2), _const_spec((A_WIDTH, A_WIDTH), 2)],
        out_specs=tm_spec,
        scratch_shapes=[pltpu.VMEM((A_WIDTH // LANES, RW_TT, LANES), F32)],
        compiler_params=_params(2),
        name="rwkv_post",
    )(yf, yb, vk, bonus, gate, lnx_g[None, :], lnx_b[None, :], _seg_matrix())


def _rope_lanes(z, cos, sin_lo, sin_hi):
    return (z * cos + pltpu.roll(z, LANES - SUBLANES, axis=1) * sin_lo + pltpu.roll(z, SUBLANES, axis=1) * sin_hi)


def _mla_q_kernel(pq_ref, gq_ref, w_ref, m_ref, gain_ref, cos_ref, slo_ref, shi_ref, q_o):
    h = _rms_rows(pq_ref[0], gq_ref[...]).astype(BF16)
    q = jnp.dot(h, w_ref[...], preferred_element_type=F32)
    ms = jnp.dot((q * q).astype(BF16), m_ref[...], preferred_element_type=F32)
    qh = q * lax.rsqrt(ms + EPS) * gain_ref[...]
    cos, slo, shi = cos_ref[...], slo_ref[...], shi_ref[...]
    tiles = [_rope_lanes(qh[:, i * HEAD_PAD:(i + 1) * HEAD_PAD], cos, slo, shi) for i in range(B_HEADS)]
    q_o[0] = (jnp.concatenate(tiles, axis=-1) * (ATTN_SCALE * LOG2_E)).astype(BF16)


def _mla_kv_kernel(pkv_ref, gkv_ref, wk_ref, wv_ref, m_ref, gain_ref, gr_ref, cos_ref, slo_ref, shi_ref,
                   place_ref, k_o, v_o):
    x = pkv_ref[0]
    h = _rms_rows(x[:, :KV_LORA], gkv_ref[...]).astype(BF16)
    kn = jnp.dot(h, wk_ref[...], preferred_element_type=F32)
    v = jnp.dot(h, wv_ref[...], preferred_element_type=F32)
    lane = lax.broadcasted_iota(jnp.int32, (v.shape[0], LANES), 1)
    tiles = []
    for pair in range(B_HEADS // 2):
        vp = v[:, pair * LANES:(pair + 1) * LANES]
        tiles += [jnp.where(lane < V_HEAD, vp, 1.0), jnp.where(lane < V_HEAD, 1.0, vp)]
    v_o[0] = jnp.concatenate(tiles, axis=-1).astype(BF16)
    ms = jnp.dot((kn * kn).astype(BF16), m_ref[...], preferred_element_type=F32)
    knh = kn * lax.rsqrt(ms + EPS) * gain_ref[...]
    kr = x[:, KV_LORA:]
    kr = kr * lax.rsqrt(jnp.sum(kr * kr, axis=-1, keepdims=True) * (1.0 / QK_ROPE) + EPS) * gr_ref[...]
    kr = _rope_lanes(kr, cos_ref[...], slo_ref[...], shi_ref[...]).astype(BF16)
    k_o[0] = (knh + jnp.dot(kr, place_ref[...], preferred_element_type=F32)).astype(BF16)


def _head_pattern(nope, rope):
    tile = jnp.concatenate([nope, rope, jnp.zeros((HEAD_PAD - QK_NOPE - QK_ROPE,), F32)])
    return jnp.tile(tile, B_HEADS)[None, :]


def _seg_mean_matrix(with_rope):
    i = jnp.arange(QP_WIDTH)
    head, off = i // HEAD_PAD, i % HEAD_PAD
    nope = off < QK_NOPE
    rope = jnp.logical_and(off >= QK_NOPE, off < QK_NOPE + QK_ROPE)
    same = head[:, None] == head[None, :]
    m = jnp.where(same & nope[:, None] & nope[None, :], 1.0 / QK_NOPE, 0.0)
    if with_rope:
        m = m + jnp.where(same & rope[:, None] & rope[None, :], 1.0 / QK_ROPE, 0.0)
    return m.astype(BF16)


def _rope_tables(n, n_ctx):
    t = jnp.arange(n, dtype=jnp.int32)
    r_pos = (t // GRID_W).astype(F32)
    c_pos = (t % GRID_W).astype(F32)
    axis_dim = QK_ROPE // 2
    inv_freq = 1.0 / (ROPE_BASE ** (jnp.arange(0, axis_dim, 2, dtype=F32) / axis_dim))
    ang_r = r_pos[:, None] * inv_freq
    ang_c = c_pos[:, None] * inv_freq
    half = axis_dim // 2
    z = jnp.zeros((n, half), F32)
    cos = jnp.concatenate([jnp.cos(ang_r)] * 2 + [jnp.cos(ang_c)] * 2, axis=1)
    s_lo = jnp.concatenate([-jnp.sin(ang_r), z, -jnp.sin(ang_c), z], axis=1)
    s_hi = jnp.concatenate([z, jnp.sin(ang_r), z, jnp.sin(ang_c)], axis=1)

    def widen(tab, fill):
        tab = jnp.concatenate([jnp.full((n_ctx, QK_ROPE), fill, F32), tab], axis=0)
        return jnp.concatenate([jnp.full((n_ctx + n, QK_NOPE), fill, F32), tab,
                                jnp.full((n_ctx + n, HEAD_PAD - QK_NOPE - QK_ROPE), fill, F32)], axis=1)

    return widen(cos, 1.0), widen(s_lo, 0.0), widen(s_hi, 0.0)


def mla_qkv(p_q, p_kv, tabs, q_norm_g, kv_norm_g, w_uq, w_ukv, q_nope_g, k_nope_g, q_rope_g, k_rope_g):
    bsz, t_all, _ = p_q.shape
    nblk = t_all // BM
    dq = QK_NOPE + QK_ROPE
    w_q = jnp.pad(w_uq.reshape(Q_LORA, B_HEADS, dq), ((0, 0), (0, 0), (0, HEAD_PAD - dq))).reshape(
        Q_LORA, QP_WIDTH).astype(BF16)
    w4 = w_ukv.reshape(KV_LORA, B_HEADS, QK_NOPE + V_HEAD)
    w_k = jnp.pad(w4[:, :, :QK_NOPE], ((0, 0), (0, 0), (0, HEAD_PAD - QK_NOPE))).reshape(
        KV_LORA, QP_WIDTH).astype(BF16)
    w_v = w4[:, :, QK_NOPE:].reshape(KV_LORA, B_WIDTH).astype(BF16)
    zero_r = jnp.zeros((QK_ROPE,), F32)
    i = jnp.arange(LANES)
    rope_lane = jnp.logical_and(i >= QK_NOPE, i < QK_NOPE + QK_ROPE)
    place = jnp.where(rope_lane[:, None] & (i[:, None] == (jnp.arange(QP_WIDTH) % HEAD_PAD)[None, :]),
                      1.0, 0.0).astype(BF16)
    gr = jnp.concatenate([jnp.zeros((QK_NOPE,), F32), k_rope_g, jnp.zeros((LANES - QK_NOPE - QK_ROPE,), F32)])

    row_spec = lambda n: pl.BlockSpec((1, BM, n), lambda j, b: (b, j, 0))
    tab_spec = pl.BlockSpec((BM, LANES), lambda j, b: (j, 0))
    c = lambda shape: _const_spec(shape, 2)
    q = pl.pallas_call(
        _mla_q_kernel,
        out_shape=jax.ShapeDtypeStruct((bsz, t_all, QP_WIDTH), BF16),
        grid=(nblk, bsz),
        in_specs=[row_spec(Q_LORA), c((1, Q_LORA)), c((Q_LORA, QP_WIDTH)), c((QP_WIDTH, QP_WIDTH)),
                  c((1, QP_WIDTH)), tab_spec, tab_spec, tab_spec],
        out_specs=row_spec(QP_WIDTH),
        compiler_params=_params(2),
        name="mla_q",
    )(p_q, q_norm_g[None, :], w_q, _seg_mean_matrix(True), _head_pattern(q_nope_g, q_rope_g), *tabs)
    k, v = pl.pallas_call(
        _mla_kv_kernel,
        out_shape=[jax.ShapeDtypeStruct((bsz, t_all, QP_WIDTH), BF16),
                   jax.ShapeDtypeStruct((bsz, t_all, QP_WIDTH), BF16)],
        grid=(nblk, bsz),
        in_specs=[row_spec(KV_IN_PAD), c((1, KV_LORA)), c((KV_LORA, QP_WIDTH)), c((KV_LORA, B_WIDTH)),
                  c((QP_WIDTH, QP_WIDTH)), c((1, QP_WIDTH)), c((1, LANES)), tab_spec, tab_spec, tab_spec,
                  c((LANES, QP_WIDTH))],
        out_specs=[row_spec(QP_WIDTH), row_spec(QP_WIDTH)],
        compiler_params=_params(2),
        name="mla_kv",
    )(p_kv, kv_norm_g[None, :], w_k, w_v, _seg_mean_matrix(False), _head_pattern(k_nope_g, zero_r),
      gr[None, :], *tabs, place)
    return q, k, v


ATT_BQ = 256
HEADS_PER_STEP = 2


def _attn_kernel(q_ref, k_ref, v_ref, o_ref):
    scores = []
    for h in range(HEADS_PER_STEP):
        q = q_ref[0, :, h * HEAD_PAD:(h + 1) * HEAD_PAD]
        k = k_ref[0, :, h * HEAD_PAD:(h + 1) * HEAD_PAD]
        scores.append(lax.dot_general(q, k, (((1,), (1,)), ((), ())), preferred_element_type=F32))
    outs = []
    for h in range(HEADS_PER_STEP):
        s = scores[h]
        p = jnp.exp2(s - jnp.max(s, axis=-1, keepdims=True)).astype(BF16)
        outs.append(jnp.dot(p, v_ref[0, :, h * HEAD_PAD:(h + 1) * HEAD_PAD], preferred_element_type=F32))
    lane = lax.broadcasted_iota(jnp.int32, outs[0].shape, 1)
    first = lane < V_HEAD
    num = jnp.where(first, outs[0], outs[1])
    den = pltpu.roll(jnp.where(first, outs[1], outs[0]), V_HEAD, axis=1)
    o_ref[0] = (num / den).astype(o_ref.dtype)


def attention(q, k, v, q_row0, nq, nk):
    bsz = q.shape[0]
    bq = min(ATT_BQ, nq)
    qb0 = q_row0 // bq
    pair = HEADS_PER_STEP * HEAD_PAD
    return pl.pallas_call(
        _attn_kernel,
        out_shape=jax.ShapeDtypeStruct((bsz, nq, B_WIDTH), BF16),
        grid=(bsz, B_HEADS // HEADS_PER_STEP, nq // bq),
        in_specs=[pl.BlockSpec((1, bq, pair), lambda b, hp, i: (b, qb0 + i, hp)),
                  pl.BlockSpec((1, nk, pair), lambda b, hp, i: (b, 0, hp)),
                  pl.BlockSpec((1, nk, pair), lambda b, hp, i: (b, 0, hp))],
        out_specs=pl.BlockSpec((1, bq, LANES), lambda b, hp, i: (b, i, hp)),
        compiler_params=_params(3),
        name="mla_attention",
    )(q, k, v)


def _outproj_kernel(ya_ref, ybc_ref, ybl_ref, pc_ref, pcp_ref, pcn_ref, x_ref, mod_ref, cw_ref, w_ref, o_ref, *,
                    row_off, n_ctx, t_all):
    j = pl.program_id(1) + row_off
    t0 = j * BM
    first = jnp.logical_or(t0 == 0, t0 == n_ctx)
    last = jnp.logical_or(t0 + BM == n_ctx, t0 + BM == t_all)
    o1, o2 = C_WIDTH, 2 * C_WIDTH
    pc = pc_ref[0]
    u = pc[:, o1:o2] * pc[:, o2:]
    pcp = pcp_ref[0, SUBLANES - 1:SUBLANES, :]
    pcn = pcn_ref[0, 0:1, :]
    u_prev = jnp.where(first, 0.0, pcp[:, o1:o2] * pcp[:, o2:])
    u_next = jnp.where(last, 0.0, pcn[:, o1:o2] * pcn[:, o2:])
    rows = lax.broadcasted_iota(jnp.int32, u.shape, 0)
    up = jnp.where(rows == 0, u_prev, pltpu.roll(u, 1, axis=0))
    un = jnp.where(rows == BM - 1, u_next, pltpu.roll(u, BM - 1, axis=0))
    yc = pc[:, :o1] * (cw_ref[0:1, :] * up + cw_ref[1:2, :] * u + cw_ref[2:3, :] * un)
    yb = jnp.where(j == 0, ybc_ref[0], ybl_ref[0]) if row_off == 0 else ybl_ref[0]
    acc = jnp.dot(ya_ref[0], w_ref[0:A_WIDTH, :], preferred_element_type=F32)
    acc += jnp.dot(yb, w_ref[A_WIDTH:A_WIDTH + B_WIDTH, :], preferred_element_type=F32)
    acc += jnp.dot(yc.astype(BF16), w_ref[A_WIDTH + B_WIDTH:, :], preferred_element_type=F32)
    o_ref[0] = x_ref[0] + mod_ref[0, :, 2 * D_MODEL:3 * D_MODEL] * acc


def outproj(ya, yb_ctx, yb_lat, p_c, xa, mod, conv_w, w_out, n_ctx, with_ctx):
    bsz, t_all, d = xa.shape
    row_off = 0 if with_ctx else n_ctx // BM
    nblk = t_all // BM - row_off
    nb8 = t_all // SUBLANES
    per8 = BM // SUBLANES
    rows = lambda n: pl.BlockSpec((1, BM, n), lambda b, j: (b, j + row_off, 0))
    lat_off = n_ctx // BM
    return pl.pallas_call(
        functools.partial(_outproj_kernel, row_off=row_off, n_ctx=n_ctx, t_all=t_all),
        out_shape=jax.ShapeDtypeStruct((bsz, nblk * BM, d), F32),
        grid=(bsz, nblk),
        in_specs=[rows(A_WIDTH),
                  pl.BlockSpec((1, BM, B_WIDTH), lambda b, j: (b, 0, 0)),
                  pl.BlockSpec((1, BM, B_WIDTH), lambda b, j: (b, jnp.maximum(j + row_off - lat_off, 0), 0)),
                  rows(C_IN),
                  pl.BlockSpec((1, SUBLANES, C_IN), lambda b, j: (b, jnp.maximum((j + row_off) * per8 - 1, 0), 0)),
                  pl.BlockSpec((1, SUBLANES, C_IN),
                               lambda b, j: (b, jnp.minimum((j + row_off + 1) * per8, nb8 - 1), 0)),
                  rows(d),
                  pl.BlockSpec((1, 1, 6 * d), lambda b, j: (2 * b + jnp.minimum(j + row_off, 1), 0, 0)),
                  _const_spec((3, C_WIDTH), 2), _const_spec((d, d), 2)],
        out_specs=pl.BlockSpec((1, BM, d), lambda b, j: (b, j, 0)),
        compiler_params=_params(2),
        name="outproj",
    )(ya, yb_ctx, yb_lat, p_c, p_c, p_c, xa, mod, conv_w, w_out)


FFN_CHUNKS = 2


def _ffn_in_kernel(x_ref, mod_ref, g_ref, w_ref, o_ref):
    sh = mod_ref[0, :, 3 * D_MODEL:4 * D_MODEL]
    sc = mod_ref[0, :, 4 * D_MODEL:5 * D_MODEL]
    h = (_rms_rows(x_ref[0], g_ref[...]) * (1.0 + sc) + sh).astype(BF16)
    d_ff = o_ref.shape[-1]
    cw = d_ff // FFN_CHUNKS
    for c in range(FFN_CHUNKS):
        gate = jnp.dot(h, w_ref[:, c * cw:(c + 1) * cw], preferred_element_type=F32)
        up = jnp.dot(h, w_ref[:, d_ff + c * cw:d_ff + (c + 1) * cw], preferred_element_type=F32)
        o_ref[0, :, c * cw:(c + 1) * cw] = (gate * jax.nn.sigmoid(gate) * up).astype(o_ref.dtype)


def _ffn_out_kernel(a_ref, x_ref, mod_ref, w_ref, o_ref):
    acc = jnp.dot(a_ref[0], w_ref[...], preferred_element_type=F32)
    o_ref[0] = x_ref[0] + mod_ref[0, :, 5 * D_MODEL:6 * D_MODEL] * acc


def ffn(x1, mod, g, w_fi, w_fo, lat_only):
    bsz, t, d = x1.shape
    d_ff = w_fo.shape[0]
    seg = 1 if lat_only else 0
    rows = lambda n: pl.BlockSpec((1, BM, n), lambda b, j: (b, j, 0))
    mod_spec = pl.BlockSpec((1, 1, 6 * d), lambda b, j: (2 * b + jnp.minimum(j + seg, 1), 0, 0))
    act = pl.pallas_call(
        _ffn_in_kernel,
        out_shape=jax.ShapeDtypeStruct((bsz, t, d_ff), BF16),
        grid=(bsz, t // BM),
        in_specs=[rows(d), mod_spec, _const_spec((1, d), 2), _const_spec((d, 2 * d_ff), 2)],
        out_specs=rows(d_ff),
        compiler_params=_params(2),
        name="ffn_in",
    )(x1, mod, g[None, :], w_fi)
    return pl.pallas_call(
        _ffn_out_kernel,
        out_shape=jax.ShapeDtypeStruct((bsz, t, d), F32),
        grid=(bsz, t // BM),
        in_specs=[rows(d_ff), rows(d), mod_spec, _const_spec((d_ff, d), 2)],
        out_specs=rows(d),
        compiler_params=_params(2),
        name="ffn_out",
    )(act, x1, mod, w_fo)


def kernel(x, c, ctx, c_ctx, ada_w, ada_b, norm1_g, norm2_g, w_in, tshift_mu, decay_w0, decay_up, icl_a0, icl_up, gate_up, k_k, k_a, r_k, lnx_g, lnx_b, q_norm_g, kv_norm_g, w_uq, w_ukv, q_nope_g, k_nope_g, q_rope_g, k_rope_g, conv_w, w_out, w_ffn_in, w_ffn_out):
    bsz, n, d = x.shape
    n_ctx = ctx.shape[1]
    depth = ada_w.shape[0]
    assert n_ctx == BM and n % BM == 0 and d == D_MODEL
    tabs = _rope_tables(n, n_ctx)

    xa = jnp.concatenate([ctx, x], axis=1)
    cond = jnp.concatenate([c, c_ctx[None, :], jnp.zeros((2 * SUBLANES - bsz - 1, d), c.dtype)], axis=0)
    silu_cond = cond * jax.nn.sigmoid(cond)
    o_kv = A_IN + Q_LORA + KV_LORA
    for l in range(depth):
        last = l == depth - 1
        ada = pmm(silu_cond, ada_w, l) + ada_b[l]
        mod = jnp.stack([jnp.broadcast_to(ada[bsz], (bsz, 6 * d)), ada[:bsz]], axis=1).reshape(2 * bsz, 1, 6 * d)

        wl = w_in[l]
        w_p = jnp.concatenate(
            [wl[:, :o_kv], jnp.zeros((d, QK_NOPE), F32), wl[:, o_kv:A_IN + B_IN],
             jnp.zeros((d, LANES - QK_NOPE - QK_ROPE), F32), wl[:, A_IN + B_IN:]], axis=1).astype(BF16)
        p_q, p_kv, p_c, r_s, k_s, v_s, dec_f, dec_b, icl_f, icl_b, vk, bonus, gate = inproj(
            xa, mod, norm1_g[l], w_p, n_ctx, tshift_mu[l], decay_w0[l], decay_up[l], icl_a0[l], icl_up[l],
            gate_up[l], k_a[l], r_k[l])
        yf, yr = wkv_scan(r_s, k_s, v_s, dec_f, dec_b, icl_f, icl_b, k_k[l], k_a[l], n_ctx)
        ya = rwkv_post(yf, yr, vk, bonus, gate, lnx_g[l], lnx_b[l])
        q, k, v = mla_qkv(p_q, p_kv, tabs, q_norm_g[l], kv_norm_g[l], w_uq[l], w_ukv[l], q_nope_g[l],
                          k_nope_g[l], q_rope_g[l], k_rope_g[l])
        yb_lat = attention(q, k, v, n_ctx, n, n_ctx + n)
        yb_ctx = yb_lat if last else attention(q, k, v, 0, n_ctx, n_ctx)

        x1 = outproj(ya, yb_ctx, yb_lat, p_c, xa, mod, conv_w[l], w_out[l].astype(BF16), n_ctx, not last)
        xa = ffn(x1, mod, norm2_g[l], w_ffn_in[l].astype(BF16), w_ffn_out[l].astype(BF16), last)
    return xa
```

```python
import functools

import jax
import jax.numpy as jnp
from jax import lax
from jax.experimental import pallas as pl
from jax.experimental.pallas import tpu as pltpu

D_MODEL = 1024
GRID_W = 64
A_HEADS = 6
A_HEAD_DIM = 64
A_WIDTH = A_HEADS * A_HEAD_DIM
DECAY_LORA = 64
ICL_LORA = 64
GATE_LORA = 128
LOG_DECAY_SCALE = 0.606531
GN_EPS = A_HEAD_DIM * 1e-5
B_HEADS = 6
QK_NOPE = 64
QK_ROPE = 32
V_HEAD = 64
B_WIDTH = B_HEADS * V_HEAD
Q_LORA = 768
KV_LORA = 256
ROPE_BASE = 10000.0
ATTN_SCALE = (QK_NOPE + QK_ROPE) ** -0.5
LOG2_E = 1.4426950408889634
C_WIDTH = D_MODEL - A_WIDTH - B_WIDTH
A_IN = 3 * A_WIDTH + DECAY_LORA + ICL_LORA + GATE_LORA
B_IN = Q_LORA + KV_LORA + QK_ROPE
C_IN = 3 * C_WIDTH
EPS = 1e-6

LANES = 128
SUBLANES = 8
HEAD_PAD = LANES
QP_WIDTH = B_HEADS * HEAD_PAD
KV_IN_PAD = KV_LORA + LANES
P_PAD = A_IN + Q_LORA + KV_IN_PAD + C_IN
VMEM_LIMIT = 48 * 1024 * 1024
BM = 256
F32 = jnp.float32
BF16 = jnp.bfloat16


def _const_spec(shape, nargs):
    zeros = tuple(0 for _ in shape)
    if nargs == 1:
        return pl.BlockSpec(shape, lambda i: zeros)
    if nargs == 2:
        return pl.BlockSpec(shape, lambda i, j: zeros)
    return pl.BlockSpec(shape, lambda i, j, k: zeros)


def _params(n_axes):
    return pltpu.CompilerParams(dimension_semantics=("arbitrary",) * n_axes, vmem_limit_bytes=VMEM_LIMIT)


def _rms_rows(x, g):
    return x * lax.rsqrt(jnp.mean(x * x, axis=-1, keepdims=True) + EPS) * g


def _mm_kernel(x_ref, w_ref, o_ref):
    o_ref[...] = jnp.dot(x_ref[...].astype(BF16), w_ref[0].astype(BF16), preferred_element_type=F32)


def pmm(x, w, layer, bn=1024):
    m, k = x.shape
    n = w.shape[2]
    return pl.pallas_call(
        _mm_kernel,
        out_shape=jax.ShapeDtypeStruct((m, n), F32),
        grid=(n // bn,),
        in_specs=[pl.BlockSpec((m, k), lambda j: (0, 0)), pl.BlockSpec((1, k, bn), lambda j: (layer, 0, j))],
        out_specs=pl.BlockSpec((m, bn), lambda j: (0, j)),
        compiler_params=_params(1),
        name="pmm",
    )(x, w)


def _seg_matrix():
    h = jnp.arange(A_WIDTH) // A_HEAD_DIM
    return (h[:, None] == h[None, :]).astype(BF16)


def _segsum(z, seg):
    hi = z.astype(BF16)
    lo = (z - hi.astype(F32)).astype(BF16)
    return jnp.dot(hi, seg, preferred_element_type=F32) + jnp.dot(lo, seg, preferred_element_type=F32)


RW_TT = 256
CHAIN_PAD = SUBLANES


def _rwkv_prep(x, prev_row, next_row, mu_ref, wlo_ref, wg_ref, bias_ref, ka_ref, rk_ref, seg_ref,
               r_o, k_o, v_o, df_o, db_o, if_o, ib_o, vk_o, bonus_o, gate_o, pk_s):
    tt = x.shape[0]
    rows = lax.broadcasted_iota(jnp.int32, x.shape, 0)
    xp = jnp.where(rows == 0, prev_row, pltpu.roll(x, 1, axis=0))
    xn = jnp.where(rows == tt - 1, next_row, pltpu.roll(x, tt - 1, axis=0))
    xs = x + mu_ref[0:1, :] * (xp - x) + mu_ref[1:2, :] * (xn - x)

    o1, o2, o3 = A_WIDTH, 2 * A_WIDTH, 3 * A_WIDTH
    o5 = o3 + DECAY_LORA + ICL_LORA
    r, k, v = xs[:, :o1], xs[:, o1:o2], xs[:, o2:o3]
    lo2 = xs[:, o3:o5]
    lane = lax.broadcasted_iota(jnp.int32, lo2.shape, 1)
    lo2 = jnp.where(lane < DECAY_LORA, jnp.tanh(lo2), lo2).astype(BF16)
    lo_out = jnp.dot(lo2, wlo_ref[...], preferred_element_type=F32) + bias_ref[...]
    gate = jnp.dot(jax.nn.sigmoid(xs[:, o5:]).astype(BF16), wg_ref[...], preferred_element_type=F32)
    dec_f, dec_b = lo_out[:, :o1], lo_out[:, o1:o2]
    icl_f, icl_b = lo_out[:, o2:o3], lo_out[:, o3:]
    m = 2.0 + (jax.nn.sigmoid(icl_f) + jax.nn.sigmoid(icl_b) - 2.0) * ka_ref[...]
    rkm = r * k * m
    seg = seg_ref[...]
    vk_o[0] = v * _segsum(rkm, seg)
    bonus_o[0] = v * _segsum(rkm * rk_ref[...], seg)
    gate_o[0] = gate
    half = tt // 2
    zero = jnp.zeros((half, LANES), F32)
    for i, (val, out) in enumerate(((r, r_o), (k, k_o), (v, v_o), (dec_f, df_o), (dec_b, db_o),
                                    (icl_f, if_o), (icl_b, ib_o))):
        heads = []
        for cb in range(A_WIDTH // LANES):
            pk_s[i, cb] = val[:, cb * LANES:(cb + 1) * LANES]
            even = pk_s[i, cb, pl.ds(0, half, stride=2), :]
            odd = pk_s[i, cb, pl.ds(1, half, stride=2), :]
            for hh in range(2):
                sl = slice(hh * A_HEAD_DIM, (hh + 1) * A_HEAD_DIM)
                heads.append(jnp.concatenate([even[:, sl], odd[:, sl]], axis=-1))
        heads += [zero] * (CHAIN_PAD - A_HEADS)
        out[0] = jnp.swapaxes(jnp.stack(heads, axis=0), 0, 1)


def _inproj_kernel(x_ref, xprev_ref, xnext_ref, mod_ref, g_ref, w_ref, *rest, n_ctx, t_all):
    prep_consts, (pq_o, pkv_o, pc_o), prep_outs = rest[:7], rest[7:10], rest[10:]
    t0 = pl.program_id(1) * BM
    first = jnp.logical_or(t0 == 0, t0 == n_ctx)
    last = jnp.logical_or(t0 + BM == n_ctx, t0 + BM == t_all)
    sh = mod_ref[0, :, 0:D_MODEL]
    sc = mod_ref[0, :, D_MODEL:2 * D_MODEL]
    x_ext = jnp.concatenate([xprev_ref[0], x_ref[0], xnext_ref[0]], axis=0)
    h_ext = (_rms_rows(x_ext, g_ref[...]) * (1.0 + sc) + sh).astype(BF16)
    pa = jnp.dot(h_ext, w_ref[:, :A_IN], preferred_element_type=F32)
    prev_row = jnp.where(first, 0.0, pa[SUBLANES - 1:SUBLANES])
    next_row = jnp.where(last, 0.0, pa[SUBLANES + BM:SUBLANES + BM + 1])
    _rwkv_prep(pa[SUBLANES:SUBLANES + BM], prev_row, next_row, *prep_consts, *prep_outs)
    h = h_ext[SUBLANES:SUBLANES + BM]
    o = A_IN
    for out in (pq_o, pkv_o, pc_o):
        n = out.shape[-1]
        out[0] = jnp.dot(h, w_ref[:, o:o + n], preferred_element_type=F32)
        o += n


def inproj(xa, mod, g, w_p, n_ctx, tshift_mu, decay_w0, decay_up, icl_a0, icl_up, gate_up, k_a, r_k):
    bsz, t_all, d = xa.shape
    nblk = t_all // BM
    zeros = jnp.zeros((DECAY_LORA, 2 * A_WIDTH), F32)
    wlo = jnp.concatenate([
        jnp.concatenate([decay_up[0], decay_up[1], zeros], axis=1),
        jnp.concatenate([zeros, icl_up[0], icl_up[1]], axis=1)], axis=0).astype(BF16)
    bias = jnp.concatenate([decay_w0[0], decay_w0[1], icl_a0[0], icl_a0[1]])[None, :]
    nb8 = t_all // SUBLANES
    per8 = BM // SUBLANES
    widths = (Q_LORA, KV_IN_PAD, C_IN)
    cm = jax.ShapeDtypeStruct((bsz, t_all // 2, CHAIN_PAD, LANES), F32)
    tm = jax.ShapeDtypeStruct((bsz, t_all, A_WIDTH), F32)
    cm_spec = pl.BlockSpec((1, BM // 2, CHAIN_PAD, LANES), lambda b, j: (b, j, 0, 0))
    rows = lambda n: pl.BlockSpec((1, BM, n), lambda b, j: (b, j, 0))
    return pl.pallas_call(
        functools.partial(_inproj_kernel, n_ctx=n_ctx, t_all=t_all),
        out_shape=[jax.ShapeDtypeStruct((bsz, t_all, n), F32) for n in widths] + [cm] * 7 + [tm] * 3,
        grid=(bsz, nblk),
        in_specs=[rows(d),
                  pl.BlockSpec((1, SUBLANES, d), lambda b, j: (b, jnp.maximum(j * per8 - 1, 0), 0)),
                  pl.BlockSpec((1, SUBLANES, d), lambda b, j: (b, jnp.minimum((j + 1) * per8, nb8 - 1), 0)),
                  pl.BlockSpec((1, 1, 6 * d), lambda b, j: (2 * b + jnp.minimum(j, 1), 0, 0)),
                  _const_spec((1, d), 2), _const_spec((d, P_PAD), 2),
                  _const_spec((2, A_IN), 2), _const_spec((2 * DECAY_LORA, 4 * A_WIDTH), 2),
                  _const_spec((GATE_LORA, A_WIDTH), 2), _const_spec((1, 4 * A_WIDTH), 2),
                  _const_spec((1, A_WIDTH), 2), _const_spec((1, A_WIDTH), 2), _const_spec((A_WIDTH, A_WIDTH), 2)],
        out_specs=[rows(n) for n in widths] + [cm_spec] * 7 + [rows(A_WIDTH)] * 3,
        scratch_shapes=[pltpu.VMEM((7, A_WIDTH // LANES, BM, LANES), F32)],
        compiler_params=_params(2),
        name="inproj",
    )(xa, xa, xa, mod, g[None, :], w_p, tshift_mu, wlo, gate_up.astype(BF16), bias, k_a[None, :],
      r_k.reshape(1, A_WIDTH), _seg_matrix())


WKV_TB = 32
WKV_ROWS = WKV_TB // 2
WKV_ACC = 2
WKV_VSPLIT = 2


def _wkv_kernel(rf_ref, rb_ref, kf_ref, kb_ref, vf_ref, vb_ref, df_ref, db_ref, if_ref, ib_ref,
                kk_ref, ka_ref, yf_ref, yb_ref,
                s_ref, a_s, b_s, k_s, rp_s, v_s, y_s, w_s, gp_s, gi_s, g_s, nat_s, *, n_chain):
    @pl.when(pl.program_id(0) == 0)
    def _():
        s_ref[...] = jnp.zeros_like(s_ref)

    lane = lax.broadcasted_iota(jnp.int32, (A_HEAD_DIM, LANES), 1)
    is_fwd = lane < n_chain
    n_batch = n_chain // CHAIN_PAD

    srcs =((rf_ref, rb_ref), (kf_ref, kb_ref), (vf_ref, vb_ref), (if_ref, ib_ref), (df_ref, db_ref))
    q_dec = len(srcs) - 1

    def to_tiles(q, j):
        f_ref, b_ref = srcs[q]
        x = jnp.concatenate([f_ref[b, j] for b in range(n_batch)] + [b_ref[b, j] for b in range(n_batch)],
                            axis=0)
        nat_s[q, 2 * j:2 * j + 2] = x.T.reshape(2, A_HEAD_DIM, LANES)

    def step_tile(q, i):
        return jnp.where(is_fwd, nat_s[q, i], nat_s[q, WKV_TB - 1 - i])

    for j in range(WKV_ROWS):
        to_tiles(q_dec, j)
    g = None
    for i in range(WKV_TB):
        w = jnp.exp(-LOG_DECAY_SCALE * jax.nn.sigmoid(step_tile(q_dec, i)))
        g = w if g is None else g * w
        w_s[i] = w
        gi_s[i] = 1.0 / g
        if i + 1 < WKV_TB:
            gp_s[i + 1] = g
    g_s[...] = g

    kk_gain = kk_ref[...]
    ka = ka_ref[...]
    for ja in range(WKV_ROWS // 2):
        jb = WKV_ROWS - 1 - ja
        for q in range(q_dec):
            to_tiles(q, ja)
            to_tiles(q, jb)
        for i in (2 * ja, 2 * ja + 1, 2 * jb, 2 * jb + 1):
            r, k, v, icl = (step_tile(q, i) for q in range(q_dec))
            icl = jax.nn.sigmoid(icl)
            g_inv = gi_s[i]
            kk = k * kk_gain
            kk = kk * lax.rsqrt(jnp.sum(kk * kk, axis=0, keepdims=True) + 1e-12)
            b = kk * icl
            rp = w_s[i] * r - kk * jnp.sum(b * r, axis=0, keepdims=True)
            v_s[i] = v
            a_s[i] = -kk if i == 0 else -kk * gp_s[i]
            rp_s[i] = rp if i == 0 else rp * gp_s[i]
            b_s[i] = b * g_inv
            k_s[i] = k * (1.0 + (icl - 1.0) * ka) * g_inv

    def step(t, carry):
        rows = A_HEAD_DIM // WKV_VSPLIT
        for part in range(WKV_VSPLIT):
            vs = slice(part * rows, (part + 1) * rows)
            sa_p = [None] * WKV_ACC
            y_p = [None] * WKV_ACC
            for kx in range(A_HEAD_DIM):
                s = s_ref[kx, vs, :]
                pa = s * a_s[t, kx:kx + 1, :]
                py = s * rp_s[t, kx:kx + 1, :]
                i = kx % WKV_ACC
                sa_p[i] = pa if sa_p[i] is None else sa_p[i] + pa
                y_p[i] = py if y_p[i] is None else y_p[i] + py
            sa = functools.reduce(lambda u, w: u + w, sa_p)
            y_s[t, vs, :] = functools.reduce(lambda u, w: u + w, y_p)
            vt = v_s[t, vs, :]
            for kx in range(A_HEAD_DIM):
                s_ref[kx, vs, :] = s_ref[kx, vs, :] + sa * b_s[t, kx:kx + 1, :] + vt * k_s[t, kx:kx + 1, :]
        return carry

    lax.fori_loop(0, WKV_TB, step, 0)

    for kx in range(A_HEAD_DIM):
        s_ref[kx] = s_ref[kx] * g_s[kx:kx + 1, :]

    for j in range(WKV_ROWS):
        y = jnp.concatenate([jnp.where(is_fwd, y_s[i], y_s[WKV_TB - 1 - i]) for i in (2 * j, 2 * j + 1)], axis=0)
        yt = y.T
        for b in range(n_batch):
            yf_ref[b, j] = yt[b * CHAIN_PAD:(b + 1) * CHAIN_PAD]
            yb_ref[b, j] = yt[n_chain + b * CHAIN_PAD:n_chain + (b + 1) * CHAIN_PAD]


def wkv_scan(r, k, v, dec_f, dec_b, icl_f, icl_b, k_k, k_a, n_ctx):
    bsz, t_half, _, _ = r.shape
    n_chain = bsz * CHAIN_PAD
    t_all, kd = 2 * t_half, A_HEAD_DIM
    assert n_ctx % WKV_TB == 0 and t_all % WKV_TB == 0 and 2 * n_chain == LANES
    nb_ctx = n_ctx // WKV_TB
    nb = t_all // WKV_TB

    def bwd_block(j):
        return jnp.where(j < nb_ctx, nb_ctx - 1 - j, nb + nb_ctx - 1 - j)

    f_spec = pl.BlockSpec((bsz, WKV_ROWS, CHAIN_PAD, LANES), lambda j: (0, j, 0, 0))
    b_spec = pl.BlockSpec((bsz, WKV_ROWS, CHAIN_PAD, LANES), lambda j: (0, bwd_block(j), 0, 0))
    tile_spec = pl.BlockSpec((kd, LANES), lambda j: (0, 0))

    def lane_tile(p):
        t = jnp.pad(p.reshape(A_HEADS, kd).T, ((0, 0), (0, CHAIN_PAD - A_HEADS)))
        return jnp.tile(t, (1, LANES // CHAIN_PAD))

    seq = pltpu.VMEM((WKV_TB, kd, LANES), F32)
    out = jax.ShapeDtypeStruct(r.shape, F32)
    yf, yb = pl.pallas_call(
        functools.partial(_wkv_kernel, n_chain=n_chain),
        out_shape=[out, out],
        grid=(nb,),
        in_specs=[f_spec, b_spec] * 5 + [tile_spec, tile_spec],
        out_specs=[f_spec, b_spec],
        scratch_shapes=[pltpu.VMEM((kd, kd, LANES), F32)] + [seq] * 9 + [
            pltpu.VMEM((kd, LANES), F32), pltpu.VMEM((5, WKV_TB, kd, LANES), F32)],
        compiler_params=_params(1),
        name="wkv_scan",
    )(r, r, k, k, v, v, dec_f, dec_b, icl_f, icl_b, lane_tile(k_k), lane_tile(k_a))
    return yf, yb


def _rwkv_post_kernel(yf_ref, yb_ref, vk_ref, bonus_ref, gate_ref, g_ref, b_ref, seg_ref, o_ref, y_s):
    half = RW_TT // 2
    yh = jnp.swapaxes(yf_ref[0] + yb_ref[0], 0, 1)
    for cb in range(A_WIDTH // LANES):
        y0 = yh[2 * cb]
        y1 = yh[2 * cb + 1]
        y_s[cb, pl.ds(0, half, stride=2), :] = jnp.concatenate([y0[:, :A_HEAD_DIM], y1[:, :A_HEAD_DIM]], axis=-1)
        y_s[cb, pl.ds(1, half, stride=2), :] = jnp.concatenate([y0[:, A_HEAD_DIM:], y1[:, A_HEAD_DIM:]], axis=-1)
    y = jnp.concatenate([y_s[cb] for cb in range(A_WIDTH // LANES)], axis=-1) + vk_ref[0]
    seg = seg_ref[...]
    mu = _segsum(y, seg) * (1.0 / A_HEAD_DIM)
    d = y - mu
    var = _segsum(d * d, seg) * (1.0 / A_HEAD_DIM)
    yn = d * lax.rsqrt(var + GN_EPS) * g_ref[...] + b_ref[...]
    o_ref[0] = ((yn + bonus_ref[0]) * gate_ref[0]).astype(o_ref.dtype)


def rwkv_post(yf, yb, vk, bonus, gate, lnx_g, lnx_b, out_dtype=BF16):
    bsz, t_all, _ = vk.shape
    cm_spec = pl.BlockSpec((1, RW_TT // 2, CHAIN_PAD, LANES), lambda b, j: (b, j, 0, 0))
    tm_spec = pl.BlockSpec((1, RW_TT, A_WIDTH), lambda b, j: (b, j, 0))
    return pl.pallas_call(
        _rwkv_post_kernel,
        out_shape=jax.ShapeDtypeStruct((bsz, t_all, A_WIDTH), out_dtype),
        grid=(bsz, t_all // RW_TT),
        in_specs=[cm_spec, cm_spec, tm_spec, tm_spec, tm_spec, _const_spec((1, A_WIDTH), 2),
                  _const_spec((1, A_WIDTH), 2), _const_spec((A_WIDTH, A_WIDTH), 2)],
        out_specs=tm_spec,
        scratch_shapes=[pltpu.VMEM((A_WIDTH // LANES, RW_TT, LANES), F32)],
        compiler_params=_params(2),
        name="rwkv_post",
    )(yf, yb, vk, bonus, gate, lnx_g[None, :], lnx_b[None, :], _seg_matrix())


def _rope_lanes(z, cos, sin_lo, sin_hi):
    return (z * cos + pltpu.roll(z, LANES - SUBLANES, axis=1) * sin_lo + pltpu.roll(z, SUBLANES, axis=1) * sin_hi)


def _mla_q_kernel(pq_ref, gq_ref, w_ref, m_ref, gain_ref, cos_ref, slo_ref, shi_ref, q_o):
    h = _rms_rows(pq_ref[0], gq_ref[...]).astype(BF16)
    q = jnp.dot(h, w_ref[...], preferred_element_type=F32)
    ms = jnp.dot((q * q).astype(BF16), m_ref[...], preferred_element_type=F32)
    qh = q * lax.rsqrt(ms + EPS) * gain_ref[...]
    cos, slo, shi = cos_ref[...], slo_ref[...], shi_ref[...]
    tiles = [_rope_lanes(qh[:, i * HEAD_PAD:(i + 1) * HEAD_PAD], cos, slo, shi) for i in range(B_HEADS)]
    q_o[0] = (jnp.concatenate(tiles, axis=-1) * (ATTN_SCALE * LOG2_E)).astype(BF16)


def _mla_kv_kernel(pkv_ref, gkv_ref, wk_ref, wv_ref, m_ref, gain_ref, gr_ref, cos_ref, slo_ref, shi_ref,
                   place_ref, k_o, v_o):
    x = pkv_ref[0]
    h = _rms_rows(x[:, :KV_LORA], gkv_ref[...]).astype(BF16)
    kn = jnp.dot(h, wk_ref[...], preferred_element_type=F32)
    v = jnp.dot(h, wv_ref[...], preferred_element_type=F32)
    lane = lax.broadcasted_iota(jnp.int32, (v.shape[0], LANES), 1)
    tiles = []
    for pair in range(B_HEADS // 2):
        vp = v[:, pair * LANES:(pair + 1) * LANES]
        tiles += [jnp.where(lane < V_HEAD, vp, 1.0), jnp.where(lane < V_HEAD, 1.0, vp)]
    v_o[0] = jnp.concatenate(tiles, axis=-1).astype(BF16)
    ms = jnp.dot((kn * kn).astype(BF16), m_ref[...], preferred_element_type=F32)
    knh = kn * lax.rsqrt(ms + EPS) * gain_ref[...]
    kr = x[:, KV_LORA:]
    kr = kr * lax.rsqrt(jnp.sum(kr * kr, axis=-1, keepdims=True) * (1.0 / QK_ROPE) + EPS) * gr_ref[...]
    kr = _rope_lanes(kr, cos_ref[...], slo_ref[...], shi_ref[...]).astype(BF16)
    k_o[0] = (knh + jnp.dot(kr, place_ref[...], preferred_element_type=F32)).astype(BF16)


def _head_pattern(nope, rope):
    tile = jnp.concatenate([nope, rope, jnp.zeros((HEAD_PAD - QK_NOPE - QK_ROPE,), F32)])
    return jnp.tile(tile, B_HEADS)[None, :]


def _seg_mean_matrix(with_rope):
    i = jnp.arange(QP_WIDTH)
    head, off = i // HEAD_PAD, i % HEAD_PAD
    nope = off < QK_NOPE
    rope = jnp.logical_and(off >= QK_NOPE, off < QK_NOPE + QK_ROPE)
    same = head[:, None] == head[None, :]
    m = jnp.where(same & nope[:, None] & nope[None, :], 1.0 / QK_NOPE, 0.0)
    if with_rope:
        m = m + jnp.where(same & rope[:, None] & rope[None, :], 1.0 / QK_ROPE, 0.0)
    return m.astype(BF16)


def _rope_tables(n, n_ctx):
    t = jnp.arange(n, dtype=jnp.int32)
    r_pos = (t // GRID_W).astype(F32)
    c_pos = (t % GRID_W).astype(F32)
    axis_dim = QK_ROPE // 2
    inv_freq = 1.0 / (ROPE_BASE ** (jnp.arange(0, axis_dim, 2, dtype=F32) / axis_dim))
    ang_r = r_pos[:, None] * inv_freq
    ang_c = c_pos[:, None] * inv_freq
    half = axis_dim // 2
    z = jnp.zeros((n, half), F32)
    cos = jnp.concatenate([jnp.cos(ang_r)] * 2 + [jnp.cos(ang_c)] * 2, axis=1)
    s_lo = jnp.concatenate([-jnp.sin(ang_r), z, -jnp.sin(ang_c), z], axis=1)
    s_hi = jnp.concatenate([z, jnp.sin(ang_r), z, jnp.sin(ang_c)], axis=1)

    def widen(tab, fill):
        tab = jnp.concatenate([jnp.full((n_ctx, QK_ROPE), fill, F32), tab], axis=0)
        return jnp.concatenate([jnp.full((n_ctx + n, QK_NOPE), fill, F32), tab,
                                jnp.full((n_ctx + n, HEAD_PAD - QK_NOPE - QK_ROPE), fill, F32)], axis=1)

    return widen(cos, 1.0), widen(s_lo, 0.0), widen(s_hi, 0.0)


def mla_qkv(p_q, p_kv, tabs, q_norm_g, kv_norm_g, w_uq, w_ukv, q_nope_g, k_nope_g, q_rope_g, k_rope_g):
    bsz, t_all, _ = p_q.shape
    nblk = t_all // BM
    dq = QK_NOPE + QK_ROPE
    w_q = jnp.pad(w_uq.reshape(Q_LORA, B_HEADS, dq), ((0, 0), (0, 0), (0, HEAD_PAD - dq))).reshape(
        Q_LORA, QP_WIDTH).astype(BF16)
    w4 = w_ukv.reshape(KV_LORA, B_HEADS, QK_NOPE + V_HEAD)
    w_k = jnp.pad(w4[:, :, :QK_NOPE], ((0, 0), (0, 0), (0, HEAD_PAD - QK_NOPE))).reshape(
        KV_LORA, QP_WIDTH).astype(BF16)
    w_v = w4[:, :, QK_NOPE:].reshape(KV_LORA, B_WIDTH).astype(BF16)
    zero_r = jnp.zeros((QK_ROPE,), F32)
    i = jnp.arange(LANES)
    rope_lane = jnp.logical_and(i >= QK_NOPE, i < QK_NOPE + QK_ROPE)
    place = jnp.where(rope_lane[:, None] & (i[:, None] == (jnp.arange(QP_WIDTH) % HEAD_PAD)[None, :]),
                      1.0, 0.0).astype(BF16)
    gr = jnp.concatenate([jnp.zeros((QK_NOPE,), F32), k_rope_g, jnp.zeros((LANES - QK_NOPE - QK_ROPE,), F32)])

    row_spec = lambda n: pl.BlockSpec((1, BM, n), lambda j, b: (b, j, 0))
    tab_spec = pl.BlockSpec((BM, LANES), lambda j, b: (j, 0))
    c = lambda shape: _const_spec(shape, 2)
    q = pl.pallas_call(
        _mla_q_kernel,
        out_shape=jax.ShapeDtypeStruct((bsz, t_all, QP_WIDTH), BF16),
        grid=(nblk, bsz),
        in_specs=[row_spec(Q_LORA), c((1, Q_LORA)), c((Q_LORA, QP_WIDTH)), c((QP_WIDTH, QP_WIDTH)),
                  c((1, QP_WIDTH)), tab_spec, tab_spec, tab_spec],
        out_specs=row_spec(QP_WIDTH),
        compiler_params=_params(2),
        name="mla_q",
    )(p_q, q_norm_g[None, :], w_q, _seg_mean_matrix(True), _head_pattern(q_nope_g, q_rope_g), *tabs)
    k, v = pl.pallas_call(
        _mla_kv_kernel,
        out_shape=[jax.ShapeDtypeStruct((bsz, t_all, QP_WIDTH), BF16),
                   jax.ShapeDtypeStruct((bsz, t_all, QP_WIDTH), BF16)],
        grid=(nblk, bsz),
        in_specs=[row_spec(KV_IN_PAD), c((1, KV_LORA)), c((KV_LORA, QP_WIDTH)), c((KV_LORA, B_WIDTH)),
                  c((QP_WIDTH, QP_WIDTH)), c((1, QP_WIDTH)), c((1, LANES)), tab_spec, tab_spec, tab_spec,
                  c((LANES, QP_WIDTH))],
        out_specs=[row_spec(QP_WIDTH), row_spec(QP_WIDTH)],
        compiler_params=_params(2),
        name="mla_kv",
    )(p_kv, kv_norm_g[None, :], w_k, w_v, _seg_mean_matrix(False), _head_pattern(k_nope_g, zero_r),
      gr[None, :], *tabs, place)
    return q, k, v


ATT_BQ = 256
HEADS_PER_STEP = 2


def _attn_kernel(q_ref, k_ref, v_ref, o_ref):
    scores = []
    for h in range(HEADS_PER_STEP):
        q = q_ref[0, :, h * HEAD_PAD:(h + 1) * HEAD_PAD]
        k = k_ref[0, :, h * HEAD_PAD:(h + 1) * HEAD_PAD]
        scores.append(lax.dot_general(q, k, (((1,), (1,)), ((), ())), preferred_element_type=F32))
    outs = []
    for h in range(HEADS_PER_STEP):
        s = scores[h]
        p = jnp.exp2(s - jnp.max(s, axis=-1, keepdims=True)).astype(BF16)
        outs.append(jnp.dot(p, v_ref[0, :, h * HEAD_PAD:(h + 1) * HEAD_PAD], preferred_element_type=F32))
    lane = lax.broadcasted_iota(jnp.int32, outs[0].shape, 1)
    first = lane < V_HEAD
    num = jnp.where(first, outs[0], outs[1])
    den = pltpu.roll(jnp.where(first, outs[1], outs[0]), V_HEAD, axis=1)
    o_ref[0] = (num / den).astype(o_ref.dtype)


def attention(q, k, v, q_row0, nq, nk):
    bsz = q.shape[0]
    bq = min(ATT_BQ, nq)
    qb0 = q_row0 // bq
    pair = HEADS_PER_STEP * HEAD_PAD
    return pl.pallas_call(
        _attn_kernel,
        out_shape=jax.ShapeDtypeStruct((bsz, nq, B_WIDTH), BF16),
        grid=(bsz, B_HEADS // HEADS_PER_STEP, nq // bq),
        in_specs=[pl.BlockSpec((1, bq, pair), lambda b, hp, i: (b, qb0 + i, hp)),
                  pl.BlockSpec((1, nk, pair), lambda b, hp, i: (b, 0, hp)),
                  pl.BlockSpec((1, nk, pair), lambda b, hp, i: (b, 0, hp))],
        out_specs=pl.BlockSpec((1, bq, LANES), lambda b, hp, i: (b, i, hp)),
        compiler_params=_params(3),
        name="mla_attention",
    )(q, k, v)


def _outproj_kernel(ya_ref, ybc_ref, ybl_ref, pc_ref, pcp_ref, pcn_ref, x_ref, mod_ref, cw_ref, w_ref, o_ref, *,
                    row_off, n_ctx, t_all):
    j = pl.program_id(1) + row_off
    t0 = j * BM
    first = jnp.logical_or(t0 == 0, t0 == n_ctx)
    last = jnp.logical_or(t0 + BM == n_ctx, t0 + BM == t_all)
    o1, o2 = C_WIDTH, 2 * C_WIDTH
    pc = pc_ref[0]
    u = pc[:, o1:o2] * pc[:, o2:]
    pcp = pcp_ref[0, SUBLANES - 1:SUBLANES, :]
    pcn = pcn_ref[0, 0:1, :]
    u_prev = jnp.where(first, 0.0, pcp[:, o1:o2] * pcp[:, o2:])
    u_next = jnp.where(last, 0.0, pcn[:, o1:o2] * pcn[:, o2:])
    rows = lax.broadcasted_iota(jnp.int32, u.shape, 0)
    up = jnp.where(rows == 0, u_prev, pltpu.roll(u, 1, axis=0))
    un = jnp.where(rows == BM - 1, u_next, pltpu.roll(u, BM - 1, axis=0))
    yc = pc[:, :o1] * (cw_ref[0:1, :] * up + cw_ref[1:2, :] * u + cw_ref[2:3, :] * un)
    yb = jnp.where(j == 0, ybc_ref[0], ybl_ref[0]) if row_off == 0 else ybl_ref[0]
    acc = jnp.dot(ya_ref[0], w_ref[0:A_WIDTH, :], preferred_element_type=F32)
    acc += jnp.dot(yb, w_ref[A_WIDTH:A_WIDTH + B_WIDTH, :], preferred_element_type=F32)
    acc += jnp.dot(yc.astype(BF16), w_ref[A_WIDTH + B_WIDTH:, :], preferred_element_type=F32)
    o_ref[0] = x_ref[0] + mod_ref[0, :, 2 * D_MODEL:3 * D_MODEL] * acc


def outproj(ya, yb_ctx, yb_lat, p_c, xa, mod, conv_w, w_out, n_ctx, with_ctx):
    bsz, t_all, d = xa.shape
    row_off = 0 if with_ctx else n_ctx // BM
    nblk = t_all // BM - row_off
    nb8 = t_all // SUBLANES
    per8 = BM // SUBLANES
    rows = lambda n: pl.BlockSpec((1, BM, n), lambda b, j: (b, j + row_off, 0))
    lat_off = n_ctx // BM
    return pl.pallas_call(
        functools.partial(_outproj_kernel, row_off=row_off, n_ctx=n_ctx, t_all=t_all),
        out_shape=jax.ShapeDtypeStruct((bsz, nblk * BM, d), F32),
        grid=(bsz, nblk),
        in_specs=[rows(A_WIDTH),
                  pl.BlockSpec((1, BM, B_WIDTH), lambda b, j: (b, 0, 0)),
                  pl.BlockSpec((1, BM, B_WIDTH), lambda b, j: (b, jnp.maximum(j + row_off - lat_off, 0), 0)),
                  rows(C_IN),
                  pl.BlockSpec((1, SUBLANES, C_IN), lambda b, j: (b, jnp.maximum((j + row_off) * per8 - 1, 0), 0)),
                  pl.BlockSpec((1, SUBLANES, C_IN),
                               lambda b, j: (b, jnp.minimum((j + row_off + 1) * per8, nb8 - 1), 0)),
                  rows(d),
                  pl.BlockSpec((1, 1, 6 * d), lambda b, j: (2 * b + jnp.minimum(j + row_off, 1), 0, 0)),
                  _const_spec((3, C_WIDTH), 2), _const_spec((d, d), 2)],
        out_specs=pl.BlockSpec((1, BM, d), lambda b, j: (b, j, 0)),
        compiler_params=_params(2),
        name="outproj",
    )(ya, yb_ctx, yb_lat, p_c, p_c, p_c, xa, mod, conv_w, w_out)


FFN_CHUNKS = 2


def _ffn_in_kernel(x_ref, mod_ref, g_ref, w_ref, o_ref):
    sh = mod_ref[0, :, 3 * D_MODEL:4 * D_MODEL]
    sc = mod_ref[0, :, 4 * D_MODEL:5 * D_MODEL]
    h = (_rms_rows(x_ref[0], g_ref[...]) * (1.0 + sc) + sh).astype(BF16)
    d_ff = o_ref.shape[-1]
    cw = d_ff // FFN_CHUNKS
    for c in range(FFN_CHUNKS):
        gate = jnp.dot(h, w_ref[:, c * cw:(c + 1) * cw], preferred_element_type=F32)
        up = jnp.dot(h, w_ref[:, d_ff + c * cw:d_ff + (c + 1) * cw], preferred_element_type=F32)
        o_ref[0, :, c * cw:(c + 1) * cw] = (gate * jax.nn.sigmoid(gate) * up).astype(o_ref.dtype)


def _ffn_out_kernel(a_ref, x_ref, mod_ref, w_ref, o_ref):
    acc = jnp.dot(a_ref[0], w_ref[...], preferred_element_type=F32)
    o_ref[0] = x_ref[0] + mod_ref[0, :, 5 * D_MODEL:6 * D_MODEL] * acc


def ffn(x1, mod, g, w_fi, w_fo, lat_only):
    bsz, t, d = x1.shape
    d_ff = w_fo.shape[0]
    seg = 1 if lat_only else 0
    rows = lambda n: pl.BlockSpec((1, BM, n), lambda b, j: (b, j, 0))
    mod_spec = pl.BlockSpec((1, 1, 6 * d), lambda b, j: (2 * b + jnp.minimum(j + seg, 1), 0, 0))
    act = pl.pallas_call(
        _ffn_in_kernel,
        out_shape=jax.ShapeDtypeStruct((bsz, t, d_ff), BF16),
        grid=(bsz, t // BM),
        in_specs=[rows(d), mod_spec, _const_spec((1, d), 2), _const_spec((d, 2 * d_ff), 2)],
        out_specs=rows(d_ff),
        compiler_params=_params(2),
        name="ffn_in",
    )(x1, mod, g[None, :], w_fi)
    return pl.pallas_call(
        _ffn_out_kernel,
        out_shape=jax.ShapeDtypeStruct((bsz, t, d), F32),
        grid=(bsz, t // BM),
        in_specs=[rows(d_ff), rows(d), mod_spec, _const_spec((d_ff, d), 2)],
        out_specs=rows(d),
        compiler_params=_params(2),
        name="ffn_out",
    )(act, x1, mod, w_fo)


def kernel(x, c, ctx, c_ctx, ada_w, ada_b, norm1_g, norm2_g, w_in, tshift_mu, decay_w0, decay_up, icl_a0, icl_up, gate_up, k_k, k_a, r_k, lnx_g, lnx_b, q_norm_g, kv_norm_g, w_uq, w_ukv, q_nope_g, k_nope_g, q_rope_g, k_rope_g, conv_w, w_out, w_ffn_in, w_ffn_out):
    bsz, n, d = x.shape
    n_ctx = ctx.shape[1]
    depth = ada_w.shape[0]
    assert n_ctx == BM and n % BM == 0 and d == D_MODEL
    tabs = _rope_tables(n, n_ctx)

    xa = jnp.concatenate([ctx, x], axis=1)
    cond = jnp.concatenate([c, c_ctx[None, :], jnp.zeros((2 * SUBLANES - bsz - 1, d), c.dtype)], axis=0)
    silu_cond = cond * jax.nn.sigmoid(cond)
    o_kv = A_IN + Q_LORA + KV_LORA
    for l in range(depth):
        last = l == depth - 1
        ada = pmm(silu_cond, ada_w, l) + ada_b[l]
        mod = jnp.stack([jnp.broadcast_to(ada[bsz], (bsz, 6 * d)), ada[:bsz]], axis=1).reshape(2 * bsz, 1, 6 * d)

        wl = w_in[l]
        w_p = jnp.concatenate(
            [wl[:, :o_kv], jnp.zeros((d, QK_NOPE), F32), wl[:, o_kv:A_IN + B_IN],
             jnp.zeros((d, LANES - QK_NOPE - QK_ROPE), F32), wl[:, A_IN + B_IN:]], axis=1).astype(BF16)
        p_q, p_kv, p_c, r_s, k_s, v_s, dec_f, dec_b, icl_f, icl_b, vk, bonus, gate = inproj(
            xa, mod, norm1_g[l], w_p, n_ctx, tshift_mu[l], decay_w0[l], decay_up[l], icl_a0[l], icl_up[l],
            gate_up[l], k_a[l], r_k[l])
        yf, yr = wkv_scan(r_s, k_s, v_s, dec_f, dec_b, icl_f, icl_b, k_k[l], k_a[l], n_ctx)
        ya = rwkv_post(yf, yr, vk, bonus, gate, lnx_g[l], lnx_b[l])
        q, k, v = mla_qkv(p_q, p_kv, tabs, q_norm_g[l], kv_norm_g[l], w_uq[l], w_ukv[l], q_nope_g[l],
                          k_nope_g[l], q_rope_g[l], k_rope_g[l])
        yb_lat = attention(q, k, v, n_ctx, n, n_ctx + n)
        yb_ctx = yb_lat if last else attention(q, k, v, 0, n_ctx, n_ctx)

        x1 = outproj(ya, yb_ctx, yb_lat, p_c, xa, mod, conv_w[l], w_out[l].astype(BF16), n_ctx, not last)
        xa = ffn(x1, mod, norm2_g[l], w_ffn_in[l].astype(BF16), w_ffn_out[l].astype(BF16), last)
    return xa
```
